```python
import math
import jax, jax.numpy as jnp
from jax import lax
import numpy as np

D_MODEL = 1024
BATCH = 16
SEQ = 2048
DEPTH = 4

MIX_WIDTH = D_MODEL
M_WIDTH = MIX_WIDTH // 2
A_WIDTH = MIX_WIDTH - M_WIDTH
M_HEADS = 4
M_HEAD_DIM = M_WIDTH // M_HEADS
A_HEADS = 4
A_V_DIM = A_WIDTH // A_HEADS
A_QK_DIM = A_V_DIM // 2
CONV_K = 4
CHUNK = 64
Q_BLOCK = 128
ROPE_THETA = 10000.0
FFN_HIDDEN = -(-8 * D_MODEL // (3 * 256)) * 256
EPS = 1e-6
HEAD_EPS = 1e-5
SPLITS = [2 * M_WIDTH,
          3 * M_WIDTH,
          4 * M_WIDTH,
          4 * M_WIDTH + 2 * M_HEADS,
          4 * M_WIDTH + 2 * M_HEADS + A_WIDTH,
          4 * M_WIDTH + 2 * M_HEADS + 2 * A_WIDTH]
IN_WIDTH = 4 * M_WIDTH + 2 * M_HEADS + 3 * A_WIDTH

kernel_name = "hymba_mlstm_diffattn_trunk"


def rmsnorm(x, g, eps=EPS):
    xf = x.astype(jnp.float32)
    y = xf * lax.rsqrt(jnp.mean(xf * xf, axis=-1, keepdims=True) + eps)
    return (y * g.astype(jnp.float32)).astype(x.dtype)


def causal_conv(u, w, b):
    k_len = w.shape[0]
    s = u.shape[1]
    up = jnp.pad(u, ((0, 0), (k_len - 1, 0), (0, 0)))
    y = up[:, 0:s] * w[0]
    for j in range(1, k_len):
        y = y + up[:, j:j + s] * w[j]
    return y + b


def rope_tables(seq, dim):
    pos = jnp.arange(seq, dtype=jnp.float32)
    inv = ROPE_THETA ** (-jnp.arange(0, dim, 2, dtype=jnp.float32) / dim)
    ang = pos[:, None] * inv[None, :]
    return jnp.cos(ang), jnp.sin(ang)


def apply_rope(t, cos, sin):
    c = cos[:, None, None, :].astype(t.dtype)
    s = sin[:, None, None, :].astype(t.dtype)
    t1, t2 = jnp.split(t, 2, axis=-1)
    return jnp.concatenate([t1 * c - t2 * s, t2 * c + t1 * s], axis=-1)


def mlstm_chunkwise(q, k, v, i_pre, log_f):
    b_sz, n_h, s_len, d = q.shape
    nc = s_len // CHUNK
    to_c = lambda t: jnp.moveaxis(t.reshape(b_sz, n_h, nc, CHUNK, *t.shape[3:]), 2, 0)
    qc, kc, vc, ic, fc = to_c(q), to_c(k), to_c(v), to_c(i_pre), to_c(log_f)
    mask = jnp.tril(jnp.ones((CHUNK, CHUNK), dtype=bool))

    def step(carry, inp):
        c_st, n_st, m_st = carry
        qq, kk, vv, ii, lf = inp
        bcum = jnp.cumsum(lf, axis=-1)
        dlog = bcum[..., :, None] - bcum[..., None, :] + ii[..., None, :]
        dlog = jnp.where(mask, dlog, -jnp.inf)
        inter = bcum + m_st[..., None]
        m_t = jnp.maximum(inter, jnp.max(dlog, axis=-1))
        w_intra = jnp.exp(dlog - m_t[..., None])
        w_inter = jnp.exp(inter - m_t)
        sc = jnp.einsum('bhtd,bhsd->bhts', qq, kk) * w_intra
        num = jnp.einsum('bhts,bhse->bhte', sc, vv) + \
            w_inter[..., None] * jnp.einsum('bhtd,bhed->bhte', qq, c_st)
        den = jnp.sum(sc, axis=-1) + w_inter * jnp.einsum('bhtd,bhd->bht', qq, n_st)
        h = num / jnp.maximum(jnp.abs(den), jnp.exp(-m_t))[..., None]
        b_last = bcum[..., -1]
        g = b_last[..., None] - bcum + ii
        m_new = jnp.maximum(b_last + m_st, jnp.max(g, axis=-1))
        ws = jnp.exp(g - m_new[..., None])
        decay = jnp.exp(b_last + m_st - m_new)
        c_new = decay[..., None, None] * c_st + jnp.einsum('bhs,bhse,bhsd->bhed', ws, vv, kk)
        n_new = decay[..., None] * n_st + jnp.einsum('bhs,bhsd->bhd', ws, kk)
        return (c_new, n_new, m_new), h

    init = (jnp.zeros((b_sz, n_h, d, d), jnp.float32),
            jnp.zeros((b_sz, n_h, d), jnp.float32),
            jnp.zeros((b_sz, n_h), jnp.float32))
    _, hs = lax.scan(step, init, (qc, kc, vc, ic, fc))
    return jnp.moveaxis(hs, 0, 2).reshape(b_sz, n_h, s_len, d)


def mlstm_mixer(qk_raw, v_raw, o_raw, gate_pre, conv_w, conv_b, g_head):
    b_sz, s_len, _ = v_raw.shape
    qk = jax.nn.silu(causal_conv(qk_raw, conv_w, conv_b))
    q, k = jnp.split(qk, 2, axis=-1)
    heads = lambda t: t.reshape(b_sz, s_len, M_HEADS, M_HEAD_DIM).transpose(0, 2, 1, 3).astype(jnp.float32)
    q = heads(q)
    k = heads(k) * (M_HEAD_DIM ** -0.5)
    v = heads(v_raw)
    i_pre, f_pre = jnp.split(gate_pre.astype(jnp.float32), 2, axis=-1)
    i_pre = i_pre.transpose(0, 2, 1)
    log_f = jax.nn.log_sigmoid(f_pre).transpose(0, 2, 1)
    h = mlstm_chunkwise(q, k, v, i_pre, log_f).transpose(0, 2, 1, 3)
    h = rmsnorm(h, g_head.reshape(M_HEADS, M_HEAD_DIM), HEAD_EPS)
    return jax.nn.sigmoid(o_raw) * h.reshape(b_sz, s_len, M_WIDTH).astype(o_raw.dtype)


def diff_attention_mixer(q_raw, k_raw, v_raw, cos, sin, lam, lam_init, g_head):
    b_sz, s_len, _ = v_raw.shape
    q = apply_rope(q_raw.reshape(b_sz, s_len, A_HEADS, 2, A_QK_DIM), cos, sin)
    k = apply_rope(k_raw.reshape(b_sz, s_len, A_HEADS, 2, A_QK_DIM), cos, sin)
    v = v_raw.reshape(b_sz, s_len, A_HEADS, A_V_DIM)
    nblk = s_len // Q_BLOCK
    qb = jnp.moveaxis(q.reshape(b_sz, nblk, Q_BLOCK, A_HEADS, 2, A_QK_DIM), 1, 0)
    key_pos = jnp.arange(s_len)
    scale = A_QK_DIM ** -0.5

    def block(args):
        qblk, bi = args
        s = jnp.einsum('bqhcd,bkhcd->bhcqk', qblk, k).astype(jnp.float32) * scale
        q_pos = bi * Q_BLOCK + jnp.arange(Q_BLOCK)
        causal = key_pos[None, :] <= q_pos[:, None]
        p = jax.nn.softmax(jnp.where(causal, s, -jnp.inf), axis=-1)
        a = p[:, :, 0] - lam * p[:, :, 1]
        return jnp.einsum('bhqk,bkhe->bqhe', a.astype(v.dtype), v)

    out = lax.map(block, (qb, jnp.arange(nblk)))
    out = jnp.moveaxis(out, 0, 1).reshape(b_sz, s_len, A_HEADS, A_V_DIM)
    out = rmsnorm(out, g_head.reshape(A_HEADS, A_V_DIM), HEAD_EPS) * (1.0 - lam_init)
    return out.reshape(b_sz, s_len, A_WIDTH)


def setup_inputs(seed: int = 0) -> dict:
    key = jax.random.key(seed)
    ks = jax.random.split(key, 20)
    nrm = lambda k, shape, s: jax.random.normal(k, shape, jnp.float32) * s
    b_i = nrm(ks[5], (DEPTH, M_HEADS), 0.1)
    b_f = 3.0 + 3.0 * jax.random.uniform(ks[6], (DEPTH, M_HEADS), jnp.float32)
    return {
        'x': nrm(ks[0], (BATCH, SEQ, D_MODEL), 1.0),
        'g_mix': 1.0 + nrm(ks[1], (DEPTH, D_MODEL), 0.02),
        'w_in': nrm(ks[2], (DEPTH, D_MODEL, IN_WIDTH), D_MODEL ** -0.5),
        'conv_w': nrm(ks[3], (DEPTH, CONV_K, 2 * M_WIDTH), CONV_K ** -0.5),
        'conv_b': nrm(ks[4], (DEPTH, 2 * M_WIDTH), 0.01),
        'b_gates': jnp.concatenate([b_i, b_f], axis=-1),
        'g_mlstm_head': 1.0 + nrm(ks[7], (DEPTH, M_WIDTH), 0.02),
        'lam_q1': nrm(ks[8], (DEPTH, A_QK_DIM), 0.1),
        'lam_k1': nrm(ks[9], (DEPTH, A_QK_DIM), 0.1),
        'lam_q2': nrm(ks[10], (DEPTH, A_QK_DIM), 0.1),
        'lam_k2': nrm(ks[11], (DEPTH, A_QK_DIM), 0.1),
        'g_diff_head': 1.0 + nrm(ks[12], (DEPTH, A_WIDTH), 0.02),
        'w_out': nrm(ks[13], (DEPTH, MIX_WIDTH, D_MODEL), MIX_WIDTH ** -0.5),
        'g_ffn': 1.0 + nrm(ks[14], (DEPTH, D_MODEL), 0.02),
        'w_gate': nrm(ks[15], (DEPTH, D_MODEL, FFN_HIDDEN), D_MODEL ** -0.5),
        'w_up': nrm(ks[16], (DEPTH, D_MODEL, FFN_HIDDEN), D_MODEL ** -0.5),
        'w_down': nrm(ks[17], (DEPTH, FFN_HIDDEN, D_MODEL), FFN_HIDDEN ** -0.5),
        'g_final': 1.0 + nrm(ks[18], (D_MODEL,), 0.02),
    }


def reference(x, g_mix, w_in, conv_w, conv_b, b_gates, g_mlstm_head, lam_q1, lam_k1,
              lam_q2, lam_k2, g_diff_head, w_out, g_ffn, w_gate, w_up, w_down, g_final):
    cos, sin = rope_tables(x.shape[1], A_QK_DIM)
    for l in range(DEPTH):
        h = rmsnorm(x, g_mix[l])
        z = h @ w_in[l]
        qk_m, v_m, o_m, gates, q_a, k_a, v_a = jnp.split(z, SPLITS, axis=-1)
        m_out = mlstm_mixer(qk_m, v_m, o_m, gates + b_gates[l], conv_w[l], conv_b[l],
                            g_mlstm_head[l])
        lam_init = 0.8 - 0.6 * math.exp(-0.3 * l)
        lam = (jnp.exp(jnp.sum(lam_q1[l].astype(jnp.float32) * lam_k1[l].astype(jnp.float32)))
               - jnp.exp(jnp.sum(lam_q2[l].astype(jnp.float32) * lam_k2[l].astype(jnp.float32)))
               + lam_init)
        a_out = diff_attention_mixer(q_a, k_a, v_a, cos, sin, lam, lam_init, g_diff_head[l])
        x = x + jnp.concatenate([m_out, a_out.astype(m_out.dtype)], axis=-1) @ w_out[l]
        h2 = rmsnorm(x, g_ffn[l])
        x = x + (jax.nn.silu(h2 @ w_gate[l]) * (h2 @ w_up[l])) @ w_down[l]
    return rmsnorm(x, g_final)
```

```python
import functools
import math

import jax
import jax.numpy as jnp
from jax import lax
from jax.experimental import pallas as pl
from jax.experimental.pallas import tpu as pltpu

D_MODEL = 1024
M_WIDTH = 512
A_WIDTH = 512
M_HEADS = 4
A_HEADS = 4
HEAD_DIM = 128
A_QK_DIM = 64
CONV_K = 4
ROPE_THETA = 10000.0
FFN_HIDDEN = 2816
EPS = 1e-6
HEAD_EPS = 1e-5
N_GATES = 2 * M_HEADS

LANES = 128
SUBLANES = 8
V7X_VMEM_BYTES = 64 * 1024 * 1024

F32 = jnp.float32
BF16 = jnp.bfloat16

ROW_TILE = 512
MLSTM_CHUNK = 128
ATTN_TQ = 256
ATTN_TK = 256
FFN_CHUNKS = ((0, 768), (768, 1536), (1536, 2304), (2304, 2816))
CONV_FILL_ROWS = 256

_SEG_QK = (0, 2 * M_WIDTH)
_SEG_VM = (_SEG_QK[1], _SEG_QK[1] + M_WIDTH)
_SEG_OM = (_SEG_VM[1], _SEG_VM[1] + M_WIDTH)
_SEG_QA = (_SEG_OM[1], _SEG_OM[1] + A_WIDTH)
_SEG_KA = (_SEG_QA[1], _SEG_QA[1] + A_WIDTH)
_SEG_VA = (_SEG_KA[1], _SEG_KA[1] + A_WIDTH)
MAIN_WIDTH = _SEG_VA[1]

_NT = (((1,), (1,)), ((), ()))


def _vmem_limit(estimate_bytes):
    return int(min(estimate_bytes * 5 // 4 + (8 << 20), V7X_VMEM_BYTES - (6 << 20)))


def _rmsnorm(x, g, eps):
    return x * lax.rsqrt(jnp.mean(x * x, axis=-1, keepdims=True) + eps) * g


def _log_sigmoid(x):
    return jnp.minimum(x, 0.0) - jnp.log1p(jnp.exp(-jnp.abs(x)))


def _rope_tile(t, cos, sin_signed, first_half):
    partner = jnp.where(first_half, pltpu.roll(t, LANES - 32, axis=1), pltpu.roll(t, 32, axis=1))
    return t * cos + partner * sin_signed


def _inproj_kernel(x_ref, g_ref, w_ref, wg_ref, bg_ref, cos_ref, sin_ref,
                   qk_ref, vm_ref, om_ref, gates_ref, qa_ref, ka_ref, va_ref):
    h = _rmsnorm(x_ref[...], g_ref[...], EPS).astype(BF16)

    def seg(bounds):
        return jnp.dot(h, w_ref[:, bounds[0]:bounds[1]], preferred_element_type=F32)

    qk_ref[...] = seg(_SEG_QK).astype(BF16)
    vm_ref[...] = seg(_SEG_VM).astype(BF16)
    om_ref[...] = seg(_SEG_OM).astype(BF16)
    va_ref[...] = seg(_SEG_VA).astype(BF16)
    gates_ref[...] = jnp.dot(h, wg_ref[...], preferred_element_type=F32) + bg_ref[...]

    cos = cos_ref[...]
    sin_signed = sin_ref[...]
    lane = lax.broadcasted_iota(jnp.int32, (1, LANES), 1)
    first_half = (lane % A_QK_DIM) < (A_QK_DIM // 2)
    q = seg(_SEG_QA)
    k = seg(_SEG_KA)
    scale = A_QK_DIM ** -0.5
    for j in range(A_WIDTH // LANES):
        sl = slice(j * LANES, (j + 1) * LANES)
        qa_ref[:, sl] = (_rope_tile(q[:, sl], cos, sin_signed, first_half) * scale).astype(BF16)
        ka_ref[:, sl] = _rope_tile(k[:, sl], cos, sin_signed, first_half).astype(BF16)


def _inproj(x, g, w_main, w_gate, b_gate, cos_t, sin_t, seq):
    rows = x.shape[0]
    tm = ROW_TILE
    pos_blocks = seq // tm
    row = lambda i: (i, 0)
    fixed = lambda i: (0, 0)
    pos = lambda i: (i % pos_blocks, 0)
    out_shapes = (
        jax.ShapeDtypeStruct((rows, 2 * M_WIDTH), BF16),
        jax.ShapeDtypeStruct((rows, M_WIDTH), BF16),
        jax.ShapeDtypeStruct((rows, M_WIDTH), BF16),
        jax.ShapeDtypeStruct((rows, LANES), F32),
        jax.ShapeDtypeStruct((rows, A_WIDTH), BF16),
        jax.ShapeDtypeStruct((rows, A_WIDTH), BF16),
        jax.ShapeDtypeStruct((rows, A_WIDTH), BF16),
    )
    out_specs = (
        pl.BlockSpec((tm, 2 * M_WIDTH), row),
        pl.BlockSpec((tm, M_WIDTH), row),
        pl.BlockSpec((tm, M_WIDTH), row),
        pl.BlockSpec((tm, LANES), row),
        pl.BlockSpec((tm, A_WIDTH), row),
        pl.BlockSpec((tm, A_WIDTH), row),
        pl.BlockSpec((tm, A_WIDTH), row),
    )
    est = (2 * (tm * D_MODEL * 4 + tm * MAIN_WIDTH * 2 + tm * LANES * 4 + 2 * tm * LANES * 4)
           + 2 * (D_MODEL * (MAIN_WIDTH + LANES) * 2) + 3 * tm * 2 * M_WIDTH * 4)
    return pl.pallas_call(
        _inproj_kernel,
        grid=(rows // tm,),
        in_specs=[
            pl.BlockSpec((tm, D_MODEL), row),
            pl.BlockSpec((1, D_MODEL), fixed),
            pl.BlockSpec((D_MODEL, MAIN_WIDTH), fixed),
            pl.BlockSpec((D_MODEL, LANES), fixed),
            pl.BlockSpec((1, LANES), fixed),
            pl.BlockSpec((tm, LANES), pos),
            pl.BlockSpec((tm, LANES), pos),
        ],
        out_specs=out_specs,
        out_shape=out_shapes,
        compiler_params=pltpu.CompilerParams(
            dimension_semantics=("arbitrary",), vmem_limit_bytes=_vmem_limit(est)),
        name="inproj",
    )(x, g, w_main, w_gate, b_gate, cos_t, sin_t)


def _mlstm_kernel(qk_ref, v_ref, o_ref, gate_ref, cw_ref, cb_ref, gh_ref, out_ref,
                  pad_scr, ct_scr, n_scr, m_scr, *, seq, chunk):
    L = chunk
    pad_scr[0:SUBLANES, :] = jnp.zeros((SUBLANES, 2 * M_WIDTH), F32)

    def fill(i, carry):
        r = pl.multiple_of(i * CONV_FILL_ROWS, CONV_FILL_ROWS)
        dst = pl.multiple_of(r + SUBLANES, SUBLANES)
        pad_scr[pl.ds(dst, CONV_FILL_ROWS), :] = qk_ref[pl.ds(r, CONV_FILL_ROWS), :].astype(F32)
        return carry

    lax.fori_loop(0, seq // CONV_FILL_ROWS, fill, 0)
    ct_scr[...] = jnp.zeros(ct_scr.shape, F32)
    n_scr[...] = jnp.zeros(n_scr.shape, F32)
    m_scr[...] = jnp.zeros(m_scr.shape, F32)

    def body(c, carry):
        r0 = pl.multiple_of(c * L, L)
        u = pad_scr[pl.ds(r0, L + SUBLANES), :]
        off = SUBLANES - (CONV_K - 1)
        acc = u[off:off + L] * cw_ref[0:1, :]
        for j in range(1, CONV_K):
            acc = acc + u[off + j:off + j + L] * cw_ref[j:j + 1, :]
        acc = acc + cb_ref[...]
        qk = acc * jax.nn.sigmoid(acc)

        gates = gate_ref[pl.ds(r0, L), :]
        log_f = _log_sigmoid(gates)
        row = lax.broadcasted_iota(jnp.int32, (L, L), 0)
        col = lax.broadcasted_iota(jnp.int32, (L, L), 1)
        tril = row >= col
        bcum = jnp.dot(tril.astype(F32), log_f, precision=lax.Precision.HIGHEST,
                       preferred_element_type=F32)
        gates_t = gates.T
        bcum_t = bcum.T

        for h in range(M_HEADS):
            hs = slice(h * HEAD_DIM, (h + 1) * HEAD_DIM)
            ks = slice(M_WIDTH + h * HEAD_DIM, M_WIDTH + (h + 1) * HEAD_DIM)
            b_col = bcum[:, M_HEADS + h:M_HEADS + h + 1]
            i_col = gates[:, h:h + 1]
            b_row = bcum_t[M_HEADS + h:M_HEADS + h + 1, :]
            i_row = gates_t[h:h + 1, :]
            m_prev = m_scr[h:h + 1, 0:1]

            dlog = jnp.where(tril, b_col - b_row + i_row, -jnp.inf)
            inter = b_col + m_prev
            m_t = jnp.maximum(inter, jnp.max(dlog, axis=-1, keepdims=True))
            w_intra = jnp.exp(dlog - m_t)
            w_inter = jnp.exp(inter - m_t)

            q = qk[:, hs]
            k = qk[:, ks] * (HEAD_DIM ** -0.5)
            qb = q.astype(BF16)
            kb = k.astype(BF16)
            vb = v_ref[pl.ds(r0, L), hs]
            sc = lax.dot_general(qb, kb, _NT, preferred_element_type=F32) * w_intra
            ct = ct_scr[h]
            n_row = n_scr[h:h + 1, :]
            num = (jnp.dot(sc.astype(BF16), vb, preferred_element_type=F32)
                   + w_inter * jnp.dot(qb, ct.astype(BF16), preferred_element_type=F32))
            den = (jnp.sum(sc, axis=-1, keepdims=True)
                   + w_inter * jnp.sum(q * n_row, axis=-1, keepdims=True))
            hh = num / jnp.maximum(jnp.abs(den), jnp.exp(-m_t))
            hn = _rmsnorm(hh, gh_ref[:, hs], HEAD_EPS)
            og = o_ref[pl.ds(r0, L), hs].astype(F32)
            out_ref[pl.ds(r0, L), hs] = (jax.nn.sigmoid(og) * hn).astype(BF16)

            b_last = b_col[L - 1:L, :]
            g_col = b_last - b_col + i_col
            m_new = jnp.maximum(b_last + m_prev, jnp.max(g_col, axis=0, keepdims=True))
            ws = jnp.exp(g_col - m_new)
            decay = jnp.exp(b_last + m_prev - m_new)
            kw = k * ws
            ct_scr[h] = decay * ct + jnp.dot(kw.T.astype(BF16), vb, preferred_element_type=F32)
            n_scr[h:h + 1, :] = decay * n_row + jnp.sum(kw, axis=0, keepdims=True)
            m_scr[h:h + 1, :] = jnp.broadcast_to(m_new, (1, LANES))
        return carry

    lax.fori_loop(0, seq // L, body, 0)


def _mlstm(qk, v, o, gates, conv_w, conv_b, g_head, batch, seq):
    per_b = lambda b: (b, 0)
    fixed = lambda b: (0, 0)
    est = (2 * seq * (2 * M_WIDTH * 2 + 3 * M_WIDTH * 2 + LANES * 4)
           + (seq + SUBLANES) * 2 * M_WIDTH * 4 + 8 * MLSTM_CHUNK * 2 * M_WIDTH * 4)
    return pl.pallas_call(
        functools.partial(_mlstm_kernel, seq=seq, chunk=MLSTM_CHUNK),
        grid=(batch,),
        in_specs=[
            pl.BlockSpec((seq, 2 * M_WIDTH), per_b),
            pl.BlockSpec((seq, M_WIDTH), per_b),
            pl.BlockSpec((seq, M_WIDTH), per_b),
            pl.BlockSpec((seq, LANES), per_b),
            pl.BlockSpec((CONV_K, 2 * M_WIDTH), fixed),
            pl.BlockSpec((1, 2 * M_WIDTH), fixed),
            pl.BlockSpec((1, M_WIDTH), fixed),
        ],
        out_specs=pl.BlockSpec((seq, M_WIDTH), per_b),
        out_shape=jax.ShapeDtypeStruct((batch * seq, M_WIDTH), BF16),
        scratch_shapes=[
            pltpu.VMEM((seq + SUBLANES, 2 * M_WIDTH), F32),
            pltpu.VMEM((M_HEADS, HEAD_DIM, HEAD_DIM), F32),
            pltpu.VMEM((SUBLANES, LANES), F32),
            pltpu.VMEM((SUBLANES, LANES), F32),
        ],
        compiler_params=pltpu.CompilerParams(
            dimension_semantics=("arbitrary",), vmem_limit_bytes=_vmem_limit(est)),
        name="mlstm",
    )(qk, v, o, gates, conv_w, conv_b, g_head)


def _attn_kernel(lam_ref, q_ref, k_ref, v_ref, gh_ref, o_ref, m_scr, l_scr, acc_scr,
                 *, tq, tk, lam_init):
    qi = pl.program_id(2)
    qf = q_ref[...].astype(F32)
    lane = lax.broadcasted_iota(jnp.int32, (1, LANES), 1)
    q_maps = tuple((qf * ((lane // A_QK_DIM) == c).astype(F32)).astype(BF16) for c in range(2))

    m_scr[...] = jnp.full(m_scr.shape, -jnp.inf, F32)
    l_scr[...] = jnp.zeros(l_scr.shape, F32)
    acc_scr[...] = jnp.zeros(acc_scr.shape, F32)

    def step(j, masked):
        k0 = pl.multiple_of(j * tk, tk)
        kb = k_ref[pl.ds(k0, tk), :]
        vb = v_ref[pl.ds(k0, tk), :]
        for c in range(2):
            s = lax.dot_general(q_maps[c], kb, _NT, preferred_element_type=F32)
            if masked:
                q_pos = qi * tq + lax.broadcasted_iota(jnp.int32, (tq, tk), 0)
                k_pos = k0 + lax.broadcasted_iota(jnp.int32, (tq, tk), 1)
                s = jnp.where(k_pos <= q_pos, s, -jnp.inf)
            m_prev = m_scr[c]
            m_new = jnp.maximum(m_prev, jnp.max(s, axis=-1, keepdims=True))
            alpha = jnp.exp(m_prev - m_new)
            p = jnp.exp(s - m_new)
            l_scr[c] = alpha * l_scr[c] + jnp.sum(p, axis=-1, keepdims=True)
            acc_scr[c] = alpha * acc_scr[c] + jnp.dot(p.astype(BF16), vb,
                                                      preferred_element_type=F32)
            m_scr[c] = m_new

    per_q = tq // tk

    def full_block(j, carry):
        step(j, False)
        return carry

    lax.fori_loop(0, qi * per_q, full_block, 0)
    for d in range(per_q):
        step(qi * per_q + d, True)

    lp = lam_ref[...]
    lam = (jnp.exp(jnp.sum(lp[0:1] * lp[1:2], axis=-1, keepdims=True))
           - jnp.exp(jnp.sum(lp[2:3] * lp[3:4], axis=-1, keepdims=True)) + lam_init)
    o = acc_scr[0] / l_scr[0] - lam * (acc_scr[1] / l_scr[1])
    o_ref[...] = (_rmsnorm(o, gh_ref[...], HEAD_EPS) * (1.0 - lam_init)).astype(BF16)


def _diff_attention(lam_pack, q, k, v, g_head, lam_init, batch, seq):
    tq, tk = ATTN_TQ, ATTN_TK
    nq = seq // tq
    return pl.pallas_call(
        functools.partial(_attn_kernel, tq=tq, tk=tk, lam_init=lam_init),
        grid=(batch, A_HEADS, nq),
        in_specs=[
            pl.BlockSpec((SUBLANES, LANES), lambda b, h, i: (0, 0)),
            pl.BlockSpec((tq, HEAD_DIM), lambda b, h, i: (b * nq + i, h)),
            pl.BlockSpec((seq, HEAD_DIM), lambda b, h, i: (b, h)),
            pl.BlockSpec((seq, HEAD_DIM), lambda b, h, i: (b, h)),
            pl.BlockSpec((1, HEAD_DIM), lambda b, h, i: (0, h)),
        ],
        out_specs=pl.BlockSpec((tq, HEAD_DIM), lambda b, h, i: (b * nq + i, h)),
        out_shape=jax.ShapeDtypeStruct((batch * seq, A_WIDTH), BF16),
        scratch_shapes=[
            pltpu.VMEM((2, tq, 1), F32),
            pltpu.VMEM((2, tq, 1), F32),
            pltpu.VMEM((2, tq, HEAD_DIM), F32),
        ],
        compiler_params=pltpu.CompilerParams(
            dimension_semantics=("arbitrary", "arbitrary", "arbitrary")),
        name="diffattn",
    )(lam_pack, q, k, v, g_head)


def _ffn_kernel(x_ref, m_ref, a_ref, wo_ref, g_ref, wg_ref, wu_ref, wd_ref, gf_ref, o_ref,
                *, final_norm):
    mix = (jnp.dot(m_ref[...], wo_ref[0:M_WIDTH, :], preferred_element_type=F32)
           + jnp.dot(a_ref[...], wo_ref[M_WIDTH:M_WIDTH + A_WIDTH, :],
                     preferred_element_type=F32))
    x1 = x_ref[...] + mix
    h2 = _rmsnorm(x1, g_ref[...], EPS).astype(BF16)
    acc = x1
    for lo, hi in FFN_CHUNKS:
        gate = jnp.dot(h2, wg_ref[:, lo:hi], preferred_element_type=F32)
        up = jnp.dot(h2, wu_ref[:, lo:hi], preferred_element_type=F32)
        act = (gate * jax.nn.sigmoid(gate) * up).astype(BF16)
        acc = acc + jnp.dot(act, wd_ref[lo:hi, :], preferred_element_type=F32)
    if final_norm:
        acc = _rmsnorm(acc, gf_ref[...], EPS)
    o_ref[...] = acc


def _ffn(x, m_out, a_out, w_out, g_ffn, w_gate, w_up, w_down, g_final, final_norm):
    rows = x.shape[0]
    tm = ROW_TILE
    row = lambda i: (i, 0)
    fixed = lambda i: (0, 0)
    resident = functools.partial(pl.BlockSpec, index_map=fixed, pipeline_mode=pl.Buffered(1))
    weights = 2 * (D_MODEL * D_MODEL + 3 * D_MODEL * FFN_HIDDEN)
    est = (weights + 2 * (2 * tm * D_MODEL * 4 + 2 * tm * M_WIDTH * 2)
           + 4 * tm * D_MODEL * 4 + 4 * tm * 768 * 4)
    return pl.pallas_call(
        functools.partial(_ffn_kernel, final_norm=final_norm),
        grid=(rows // tm,),
        in_specs=[
            pl.BlockSpec((tm, D_MODEL), row),
            pl.BlockSpec((tm, M_WIDTH), row),
            pl.BlockSpec((tm, A_WIDTH), row),
            resident((D_MODEL, D_MODEL)),
            pl.BlockSpec((1, D_MODEL), fixed),
            resident((D_MODEL, FFN_HIDDEN)),
            resident((D_MODEL, FFN_HIDDEN)),
            resident((FFN_HIDDEN, D_MODEL)),
            pl.BlockSpec((1, D_MODEL), fixed),
        ],
        out_specs=pl.BlockSpec((tm, D_MODEL), row),
        out_shape=jax.ShapeDtypeStruct((rows, D_MODEL), F32),
        compiler_params=pltpu.CompilerParams(
            dimension_semantics=("arbitrary",), vmem_limit_bytes=_vmem_limit(est)),
        name="ffn",
    )(x, m_out, a_out, w_out, g_ffn, w_gate, w_up, w_down, g_final)


def _rope_lane_tables(seq):
    pos = jnp.arange(seq, dtype=F32)
    inv = ROPE_THETA ** (-jnp.arange(0, A_QK_DIM, 2, dtype=F32) / A_QK_DIM)
    ang = pos[:, None] * inv[None, :]
    cos, sin = jnp.cos(ang), jnp.sin(ang)
    reps = LANES // A_QK_DIM
    cos_t = jnp.tile(cos, (1, 2 * reps))
    sin_t = jnp.tile(jnp.concatenate([-sin, sin], axis=1), (1, reps))
    return cos_t, sin_t


def _pad_lanes(a):
    return jnp.pad(a, ((0, 0), (0, LANES - a.shape[-1])))


def kernel(x, g_mix, w_in, conv_w, conv_b, b_gates, g_mlstm_head, lam_q1, lam_k1, lam_q2,
           lam_k2, g_diff_head, w_out, g_ffn, w_gate, w_up, w_down, g_final):
    batch, seq, d_model = x.shape
    depth = w_in.shape[0]
    assert d_model == D_MODEL and seq % ROW_TILE == 0 and seq % MLSTM_CHUNK == 0
    cos_t, sin_t = _rope_lane_tables(seq)
    gate_lo = 4 * M_WIDTH
    gate_hi = gate_lo + N_GATES

    xf = x.reshape(batch * seq, d_model)
    for l in range(depth):
        w_l = w_in[l]
        w_main = jnp.concatenate([w_l[:, :gate_lo], w_l[:, gate_hi:]], axis=1).astype(BF16)
        w_g = _pad_lanes(w_l[:, gate_lo:gate_hi]).astype(BF16)
        b_g = _pad_lanes(b_gates[l][None, :])
        qk_m, v_m, o_m, gates, q_a, k_a, v_a = _inproj(
            xf, g_mix[l][None, :], w_main, w_g, b_g, cos_t, sin_t, seq)

        m_out = _mlstm(qk_m, v_m, o_m, gates, conv_w[l], conv_b[l][None, :],
                       g_mlstm_head[l][None, :], batch, seq)

        lam_init = 0.8 - 0.6 * math.exp(-0.3 * l)
        lam_pack = jnp.pad(jnp.stack([lam_q1[l], lam_k1[l], lam_q2[l], lam_k2[l]]),
                           ((0, SUBLANES - 4), (0, LANES - A_QK_DIM)))
        a_out = _diff_attention(lam_pack, q_a, k_a, v_a, g_diff_head[l][None, :], lam_init,
                                batch, seq)

        xf = _ffn(xf, m_out, a_out, w_out[l].astype(BF16), g_ffn[l][None, :],
                  w_gate[l].astype(BF16), w_up[l].astype(BF16), w_down[l].astype(BF16),
                  g_final[None, :], final_norm=(l == depth - 1))
    return xf.reshape(batch, seq, d_model)
```

```python
import functools
import math

import jax
import jax.numpy as jnp
from jax import lax
from jax.experimental import pallas as pl
from jax.experimental.pallas import tpu as pltpu

D_MODEL = 1024
M_WIDTH = 512
A_WIDTH = 512
M_HEADS = 4
A_HEADS = 4
HEAD_DIM = 128
A_QK_DIM = 64
CONV_K = 4
ROPE_THETA = 10000.0
FFN_HIDDEN = 2816
EPS = 1e-6
HEAD_EPS = 1e-5
N_GATES = 2 * M_HEADS

LANES = 128
SUBLANES = 8
V7X_VMEM_BYTES = 64 * 1024 * 1024

F32 = jnp.float32
BF16 = jnp.bfloat16

ROW_TILE = 512
MLSTM_CHUNK = 128
ATTN_TQ = 256
FFN_CHUNKS = ((0, 768), (768, 1536), (1536, 2304), (2304, 2816))
CONV_FILL_ROWS = 256

_SEG_QK = (0, 2 * M_WIDTH)
_SEG_VM = (_SEG_QK[1], _SEG_QK[1] + M_WIDTH)
_SEG_OM = (_SEG_VM[1], _SEG_VM[1] + M_WIDTH)
_SEG_QA = (_SEG_OM[1], _SEG_OM[1] + A_WIDTH)
_SEG_KA = (_SEG_QA[1], _SEG_QA[1] + A_WIDTH)
_SEG_VA = (_SEG_KA[1], _SEG_KA[1] + A_WIDTH)
MAIN_WIDTH = _SEG_VA[1]

_NT = (((1,), (1,)), ((), ()))


def _vmem_limit(estimate_bytes):
    return int(min(estimate_bytes * 5 // 4 + (8 << 20), V7X_VMEM_BYTES - (6 << 20)))


def _rmsnorm(x, g, eps):
    return x * lax.rsqrt(jnp.mean(x * x, axis=-1, keepdims=True) + eps) * g


def _log_sigmoid(x):
    return jnp.minimum(x, 0.0) - jnp.log1p(jnp.exp(-jnp.abs(x)))


def _rope_tile(t, cos, sin_signed, first_half):
    partner = jnp.where(first_half, pltpu.roll(t, LANES - 32, axis=1), pltpu.roll(t, 32, axis=1))
    return t * cos + partner * sin_signed


def _inproj_kernel(x_ref, g_ref, w_ref, wg_ref, bg_ref, cos_ref, sin_ref,
                   qk_ref, vm_ref, om_ref, gates_ref, qa_ref, ka_ref, va_ref):
    h = _rmsnorm(x_ref[...], g_ref[...], EPS).astype(BF16)

    def seg(bounds):
        return jnp.dot(h, w_ref[:, bounds[0]:bounds[1]], preferred_element_type=F32)

    qk_ref[...] = seg(_SEG_QK).astype(BF16)
    vm_ref[...] = seg(_SEG_VM).astype(BF16)
    om_ref[...] = seg(_SEG_OM).astype(BF16)
    va_ref[...] = seg(_SEG_VA).astype(BF16)
    gates_ref[...] = jnp.dot(h, wg_ref[...], preferred_element_type=F32) + bg_ref[...]

    cos = cos_ref[...]
    sin_signed = sin_ref[...]
    lane = lax.broadcasted_iota(jnp.int32, (1, LANES), 1)
    first_half = (lane % A_QK_DIM) < (A_QK_DIM // 2)
    q = seg(_SEG_QA)
    k = seg(_SEG_KA)
    scale = A_QK_DIM ** -0.5
    for j in range(A_WIDTH // LANES):
        sl = slice(j * LANES, (j + 1) * LANES)
        qa_ref[:, sl] = (_rope_tile(q[:, sl], cos, sin_signed, first_half) * scale).astype(BF16)
        ka_ref[:, sl] = _rope_tile(k[:, sl], cos, sin_signed, first_half).astype(BF16)


def _inproj(x, g, w_main, w_gate, b_gate, cos_t, sin_t, seq):
    rows = x.shape[0]
    tm = ROW_TILE
    pos_blocks = seq // tm
    row = lambda i: (i, 0)
    fixed = lambda i: (0, 0)
    pos = lambda i: (i % pos_blocks, 0)
    out_shapes = (
        jax.ShapeDtypeStruct((rows, 2 * M_WIDTH), BF16),
        jax.ShapeDtypeStruct((rows, M_WIDTH), BF16),
        jax.ShapeDtypeStruct((rows, M_WIDTH), BF16),
        jax.ShapeDtypeStruct((rows, LANES), F32),
        jax.ShapeDtypeStruct((rows, A_WIDTH), BF16),
        jax.ShapeDtypeStruct((rows, A_WIDTH), BF16),
        jax.ShapeDtypeStruct((rows, A_WIDTH), BF16),
    )
    out_specs = (
        pl.BlockSpec((tm, 2 * M_WIDTH), row),
        pl.BlockSpec((tm, M_WIDTH), row),
        pl.BlockSpec((tm, M_WIDTH), row),
        pl.BlockSpec((tm, LANES), row),
        pl.BlockSpec((tm, A_WIDTH), row),
        pl.BlockSpec((tm, A_WIDTH), row),
        pl.BlockSpec((tm, A_WIDTH), row),
    )
    est = (2 * (tm * D_MODEL * 4 + tm * MAIN_WIDTH * 2 + tm * LANES * 4 + 2 * tm * LANES * 4)
           + 2 * (D_MODEL * (MAIN_WIDTH + LANES) * 2) + 3 * tm * 2 * M_WIDTH * 4)
    return pl.pallas_call(
        _inproj_kernel,
        grid=(rows // tm,),
        in_specs=[
            pl.BlockSpec((tm, D_MODEL), row),
            pl.BlockSpec((1, D_MODEL), fixed),
            pl.BlockSpec((D_MODEL, MAIN_WIDTH), fixed),
            pl.BlockSpec((D_MODEL, LANES), fixed),
            pl.BlockSpec((1, LANES), fixed),
            pl.BlockSpec((tm, LANES), pos),
            pl.BlockSpec((tm, LANES), pos),
        ],
        out_specs=out_specs,
        out_shape=out_shapes,
        compiler_params=pltpu.CompilerParams(
            dimension_semantics=("arbitrary",), vmem_limit_bytes=_vmem_limit(est)),
        name="inproj",
    )(x, g, w_main, w_gate, b_gate, cos_t, sin_t)


def _mlstm_kernel(qk_ref, v_ref, o_ref, gate_ref, cw_ref, cb_ref, gh_ref, out_ref,
                  pad_scr, ct_scr, n_scr, m_scr, *, seq, chunk):
    L = chunk
    pad_scr[0:SUBLANES, :] = jnp.zeros((SUBLANES, 2 * M_WIDTH), F32)

    def fill(i, carry):
        r = pl.multiple_of(i * CONV_FILL_ROWS, CONV_FILL_ROWS)
        dst = pl.multiple_of(r + SUBLANES, SUBLANES)
        pad_scr[pl.ds(dst, CONV_FILL_ROWS), :] = qk_ref[pl.ds(r, CONV_FILL_ROWS), :].astype(F32)
        return carry

    lax.fori_loop(0, seq // CONV_FILL_ROWS, fill, 0)
    ct_scr[...] = jnp.zeros(ct_scr.shape, F32)
    n_scr[...] = jnp.zeros(n_scr.shape, F32)
    m_scr[...] = jnp.zeros(m_scr.shape, F32)

    def body(c, carry):
        r0 = pl.multiple_of(c * L, L)
        u = pad_scr[pl.ds(r0, L + SUBLANES), :]
        off = SUBLANES - (CONV_K - 1)
        acc = u[off:off + L] * cw_ref[0:1, :]
        for j in range(1, CONV_K):
            acc = acc + u[off + j:off + j + L] * cw_ref[j:j + 1, :]
        acc = acc + cb_ref[...]
        qk = acc * jax.nn.sigmoid(acc)

        gates = gate_ref[pl.ds(r0, L), :]
        log_f = _log_sigmoid(gates)
        row = lax.broadcasted_iota(jnp.int32, (L, L), 0)
        col = lax.broadcasted_iota(jnp.int32, (L, L), 1)
        tril = row >= col
        bcum = jnp.dot(tril.astype(F32), log_f, precision=lax.Precision.HIGHEST,
                       preferred_element_type=F32)
        gates_t = gates.T
        bcum_t = bcum.T

        for h in range(M_HEADS):
            hs = slice(h * HEAD_DIM, (h + 1) * HEAD_DIM)
            ks = slice(M_WIDTH + h * HEAD_DIM, M_WIDTH + (h + 1) * HEAD_DIM)
            b_col = bcum[:, M_HEADS + h:M_HEADS + h + 1]
            i_col = gates[:, h:h + 1]
            b_row = bcum_t[M_HEADS + h:M_HEADS + h + 1, :]
            i_row = gates_t[h:h + 1, :]
            m_prev = m_scr[h:h + 1, 0:1]

            dlog = jnp.where(tril, b_col - b_row + i_row, -jnp.inf)
            inter = b_col + m_prev
            m_t = jnp.maximum(inter, jnp.max(dlog, axis=-1, keepdims=True))
            w_intra = jnp.exp(dlog - m_t)
            w_inter = jnp.exp(inter - m_t)

            q = qk[:, hs]
            k = qk[:, ks] * (HEAD_DIM ** -0.5)
            qb = q.astype(BF16)
            kb = k.astype(BF16)
            vb = v_ref[pl.ds(r0, L), hs]
            sc = lax.dot_general(qb, kb, _NT, preferred_element_type=F32) * w_intra
            ct = ct_scr[h]
            n_row = n_scr[h:h + 1, :]
            num = (jnp.dot(sc.astype(BF16), vb, preferred_element_type=F32)
                   + w_inter * jnp.dot(qb, ct.astype(BF16), preferred_element_type=F32))
            den = (jnp.sum(sc, axis=-1, keepdims=True)
                   + w_inter * jnp.sum(q * n_row, axis=-1, keepdims=True))
            hh = num / jnp.maximum(jnp.abs(den), jnp.exp(-m_t))
            hn = _rmsnorm(hh, gh_ref[:, hs], HEAD_EPS)
            og = o_ref[pl.ds(r0, L), hs].astype(F32)
            out_ref[pl.ds(r0, L), hs] = (jax.nn.sigmoid(og) * hn).astype(BF16)

            b_last = b_col[L - 1:L, :]
            g_col = b_last - b_col + i_col
            m_new = jnp.maximum(b_last + m_prev, jnp.max(g_col, axis=0, keepdims=True))
            ws = jnp.exp(g_col - m_new)
            decay = jnp.exp(b_last + m_prev - m_new)
            kw = k * ws
            ct_scr[h] = decay * ct + jnp.dot(kw.T.astype(BF16), vb, preferred_element_type=F32)
            n_scr[h:h + 1, :] = decay * n_row + jnp.sum(kw, axis=0, keepdims=True)
            m_scr[h:h + 1, :] = jnp.broadcast_to(m_new, (1, LANES))
        return carry

    lax.fori_loop(0, seq // L, body, 0)


def _mlstm(qk, v, o, gates, conv_w, conv_b, g_head, batch, seq):
    per_b = lambda b: (b, 0)
    fixed = lambda b: (0, 0)
    est = (2 * seq * (2 * M_WIDTH * 2 + 3 * M_WIDTH * 2 + LANES * 4)
           + (seq + SUBLANES) * 2 * M_WIDTH * 4 + 8 * MLSTM_CHUNK * 2 * M_WIDTH * 4)
    return pl.pallas_call(
        functools.partial(_mlstm_kernel, seq=seq, chunk=MLSTM_CHUNK),
        grid=(batch,),
        in_specs=[
            pl.BlockSpec((seq, 2 * M_WIDTH), per_b),
            pl.BlockSpec((seq, M_WIDTH), per_b),
            pl.BlockSpec((seq, M_WIDTH), per_b),
            pl.BlockSpec((seq, LANES), per_b),
            pl.BlockSpec((CONV_K, 2 * M_WIDTH), fixed),
            pl.BlockSpec((1, 2 * M_WIDTH), fixed),
            pl.BlockSpec((1, M_WIDTH), fixed),
        ],
        out_specs=pl.BlockSpec((seq, M_WIDTH), per_b),
        out_shape=jax.ShapeDtypeStruct((batch * seq, M_WIDTH), BF16),
        scratch_shapes=[
            pltpu.VMEM((seq + SUBLANES, 2 * M_WIDTH), F32),
            pltpu.VMEM((M_HEADS, HEAD_DIM, HEAD_DIM), F32),
            pltpu.VMEM((SUBLANES, LANES), F32),
            pltpu.VMEM((SUBLANES, LANES), F32),
        ],
        compiler_params=pltpu.CompilerParams(
            dimension_semantics=("arbitrary",), vmem_limit_bytes=_vmem_limit(est)),
        name="mlstm",
    )(qk, v, o, gates, conv_w, conv_b, g_head)


def _attn_kernel(lam_ref, q_ref, k_ref, v_ref, gh_ref, o_ref, *, seq, tq, lam_init):
    lane = lax.broadcasted_iota(jnp.int32, (1, LANES), 1)
    map_masks = tuple(((lane // A_QK_DIM) == c).astype(F32) for c in range(2))
    row = lax.broadcasted_iota(jnp.int32, (tq, tq), 0)
    col = lax.broadcasted_iota(jnp.int32, (tq, tq), 1)
    causal = col <= row
    lp = lam_ref[...]
    lam = (jnp.exp(jnp.sum(lp[0:1] * lp[1:2], axis=-1, keepdims=True))
           - jnp.exp(jnp.sum(lp[2:3] * lp[3:4], axis=-1, keepdims=True)) + lam_init)
    gh = gh_ref[...]

    for qi in range(seq // tq):
        r0 = qi * tq
        qf = q_ref[r0:r0 + tq, :].astype(F32)
        k_diag = k_ref[r0:r0 + tq, :]
        v_diag = v_ref[r0:r0 + tq, :]
        maps = []
        for c in range(2):
            qc = (qf * map_masks[c]).astype(BF16)
            s_d = jnp.where(causal,
                            lax.dot_general(qc, k_diag, _NT, preferred_element_type=F32),
                            -jnp.inf)
            m = jnp.max(s_d, axis=-1, keepdims=True)
            if qi > 0:
                s_p = lax.dot_general(qc, k_ref[0:r0, :], _NT, preferred_element_type=F32)
                m = jnp.maximum(m, jnp.max(s_p, axis=-1, keepdims=True))
            p_d = jnp.exp(s_d - m)
            l = jnp.sum(p_d, axis=-1, keepdims=True)
            acc = jnp.dot(p_d.astype(BF16), v_diag, preferred_element_type=F32)
            if qi > 0:
                p_p = jnp.exp(s_p - m)
                l = l + jnp.sum(p_p, axis=-1, keepdims=True)
                acc = acc + jnp.dot(p_p.astype(BF16), v_ref[0:r0, :],
                                    preferred_element_type=F32)
            maps.append(acc / l)
        o = maps[0] - lam * maps[1]
        o_ref[r0:r0 + tq, :] = (_rmsnorm(o, gh, HEAD_EPS) * (1.0 - lam_init)).astype(BF16)


def _diff_attention(lam_pack, q, k, v, g_head, lam_init, batch, seq):
    per_head = lambda b, h: (b, h)
    est = 2 * 4 * seq * HEAD_DIM * 2 + 8 * ATTN_TQ * seq * 4
    return pl.pallas_call(
        functools.partial(_attn_kernel, seq=seq, tq=ATTN_TQ, lam_init=lam_init),
        grid=(batch, A_HEADS),
        in_specs=[
            pl.BlockSpec((SUBLANES, LANES), lambda b, h: (0, 0)),
            pl.BlockSpec((seq, HEAD_DIM), per_head),
            pl.BlockSpec((seq, HEAD_DIM), per_head),
            pl.BlockSpec((seq, HEAD_DIM), per_head),
            pl.BlockSpec((1, HEAD_DIM), lambda b, h: (0, h)),
        ],
        out_specs=pl.BlockSpec((seq, HEAD_DIM), per_head),
        out_shape=jax.ShapeDtypeStruct((batch * seq, A_WIDTH), BF16),
        compiler_params=pltpu.CompilerParams(
            dimension_semantics=("arbitrary", "arbitrary"), vmem_limit_bytes=_vmem_limit(est)),
        name="diffattn",
    )(lam_pack, q, k, v, g_head)


def _ffn_kernel(x_ref, m_ref, a_ref, wo_ref, g_ref, wg_ref, wu_ref, wd_ref, gf_ref, o_ref,
                *, final_norm):
    mix = (jnp.dot(m_ref[...], wo_ref[0:M_WIDTH, :], preferred_element_type=F32)
           + jnp.dot(a_ref[...], wo_ref[M_WIDTH:M_WIDTH + A_WIDTH, :],
                     preferred_element_type=F32))
    x1 = x_ref[...] + mix
    h2 = _rmsnorm(x1, g_ref[...], EPS).astype(BF16)
    acc = x1
    for lo, hi in FFN_CHUNKS:
        gate = jnp.dot(h2, wg_ref[:, lo:hi], preferred_element_type=F32)
        up = jnp.dot(h2, wu_ref[:, lo:hi], preferred_element_type=F32)
        act = (gate * jax.nn.sigmoid(gate) * up).astype(BF16)
        acc = acc + jnp.dot(act, wd_ref[lo:hi, :], preferred_element_type=F32)
    if final_norm:
        acc = _rmsnorm(acc, gf_ref[...], EPS)
    o_ref[...] = acc


def _ffn(x, m_out, a_out, w_out, g_ffn, w_gate, w_up, w_down, g_final, final_norm):
    rows = x.shape[0]
    tm = ROW_TILE
    row = lambda i: (i, 0)
    fixed = lambda i: (0, 0)
    resident = functools.partial(pl.BlockSpec, index_map=fixed, pipeline_mode=pl.Buffered(1))
    weights = 2 * (D_MODEL * D_MODEL + 3 * D_MODEL * FFN_HIDDEN)
    est = (weights + 2 * (2 * tm * D_MODEL * 4 + 2 * tm * M_WIDTH * 2)
           + 4 * tm * D_MODEL * 4 + 4 * tm * 768 * 4)
    return pl.pallas_call(
        functools.partial(_ffn_kernel, final_norm=final_norm),
        grid=(rows // tm,),
        in_specs=[
            pl.BlockSpec((tm, D_MODEL), row),
            pl.BlockSpec((tm, M_WIDTH), row),
            pl.BlockSpec((tm, A_WIDTH), row),
            resident((D_MODEL, D_MODEL)),
            pl.BlockSpec((1, D_MODEL), fixed),
            resident((D_MODEL, FFN_HIDDEN)),
            resident((D_MODEL, FFN_HIDDEN)),
            resident((FFN_HIDDEN, D_MODEL)),
            pl.BlockSpec((1, D_MODEL), fixed),
        ],
        out_specs=pl.BlockSpec((tm, D_MODEL), row),
        out_shape=jax.ShapeDtypeStruct((rows, D_MODEL), F32),
        compiler_params=pltpu.CompilerParams(
            dimension_semantics=("arbitrary",), vmem_limit_bytes=_vmem_limit(est)),
        name="ffn",
    )(x, m_out, a_out, w_out, g_ffn, w_gate, w_up, w_down, g_final)


def _rope_lane_tables(seq):
    pos = jnp.arange(seq, dtype=F32)
    inv = ROPE_THETA ** (-jnp.arange(0, A_QK_DIM, 2, dtype=F32) / A_QK_DIM)
    ang = pos[:, None] * inv[None, :]
    cos, sin = jnp.cos(ang), jnp.sin(ang)
    reps = LANES // A_QK_DIM
    cos_t = jnp.tile(cos, (1, 2 * reps))
    sin_t = jnp.tile(jnp.concatenate([-sin, sin], axis=1), (1, reps))
    return cos_t, sin_t


def _pad_lanes(a):
    return jnp.pad(a, ((0, 0), (0, LANES - a.shape[-1])))


def kernel(x, g_mix, w_in, conv_w, conv_b, b_gates, g_mlstm_head, lam_q1, lam_k1, lam_q2,
           lam_k2, g_diff_head, w_out, g_ffn, w_gate, w_up, w_down, g_final):
    batch, seq, d_model = x.shape
    depth = w_in.shape[0]
    assert d_model == D_MODEL and seq % ROW_TILE == 0 and seq % MLSTM_CHUNK == 0
    cos_t, sin_t = _rope_lane_tables(seq)
    gate_lo = 4 * M_WIDTH
    gate_hi = gate_lo + N_GATES

    xf = x.reshape(batch * seq, d_model)
    for l in range(depth):
        w_l = w_in[l]
        w_main = jnp.concatenate([w_l[:, :gate_lo], w_l[:, gate_hi:]], axis=1).astype(BF16)
        w_g = _pad_lanes(w_l[:, gate_lo:gate_hi]).astype(BF16)
        b_g = _pad_lanes(b_gates[l][None, :])
        qk_m, v_m, o_m, gates, q_a, k_a, v_a = _inproj(
            xf, g_mix[l][None, :], w_main, w_g, b_g, cos_t, sin_t, seq)

        m_out = _mlstm(qk_m, v_m, o_m, gates, conv_w[l], conv_b[l][None, :],
                       g_mlstm_head[l][None, :], batch, seq)

        lam_init = 0.8 - 0.6 * math.exp(-0.3 * l)
        lam_pack = jnp.pad(jnp.stack([lam_q1[l], lam_k1[l], lam_q2[l], lam_k2[l]]),
                           ((0, SUBLANES - 4), (0, LANES - A_QK_DIM)))
        a_out = _diff_attention(lam_pack, q_a, k_a, v_a, g_diff_head[l][None, :], lam_init,
                                batch, seq)

        xf = _ffn(xf, m_out, a_out, w_out[l].astype(BF16), g_ffn[l][None, :],
                  w_gate[l].astype(BF16), w_up[l].astype(BF16), w_down[l].astype(BF16),
                  g_final[None, :], final_norm=(l == depth - 1))
    return xf.reshape(batch, seq, d_model)
```

```python
import functools
import math

import jax
import jax.numpy as jnp
from jax import lax
from jax.experimental import pallas as pl
from jax.experimental.pallas import tpu as pltpu

D_MODEL = 1024
M_WIDTH = 512
A_WIDTH = 512
M_HEADS = 4
A_HEADS = 4
HEAD_DIM = 128
A_QK_DIM = 64
CONV_K = 4
ROPE_THETA = 10000.0
FFN_HIDDEN = 2816
EPS = 1e-6
HEAD_EPS = 1e-5
N_GATES = 2 * M_HEADS

LANES = 128
SUBLANES = 8
V7X_VMEM_BYTES = 64 * 1024 * 1024

F32 = jnp.float32
BF16 = jnp.bfloat16

ROW_TILE = 512
MLSTM_CHUNK = 128
ATTN_TQ = 512
FFN_CHUNKS = ((0, 768), (768, 1536), (1536, 2304), (2304, 2816))
CONV_FILL_ROWS = 256

_SEG_QK = (0, 2 * M_WIDTH)
_SEG_VM = (_SEG_QK[1], _SEG_QK[1] + M_WIDTH)
_SEG_OM = (_SEG_VM[1], _SEG_VM[1] + M_WIDTH)
_SEG_QA = (_SEG_OM[1], _SEG_OM[1] + A_WIDTH)
_SEG_KA = (_SEG_QA[1], _SEG_QA[1] + A_WIDTH)
_SEG_VA = (_SEG_KA[1], _SEG_KA[1] + A_WIDTH)
MAIN_WIDTH = _SEG_VA[1]

_NT = (((1,), (1,)), ((), ()))


def _vmem_limit(estimate_bytes):
    return int(min(estimate_bytes * 5 // 4 + (8 << 20), V7X_VMEM_BYTES - (6 << 20)))


def _rmsnorm(x, g, eps):
    return x * lax.rsqrt(jnp.mean(x * x, axis=-1, keepdims=True) + eps) * g


def _log_sigmoid(x):
    return jnp.minimum(x, 0.0) - jnp.log1p(jnp.exp(-jnp.abs(x)))


def _rope_tile(t, cos, sin_signed, first_half):
    partner = jnp.where(first_half, pltpu.roll(t, LANES - 32, axis=1), pltpu.roll(t, 32, axis=1))
    return t * cos + partner * sin_signed


def _inproj_kernel(x_ref, g_ref, w_ref, wg_ref, bg_ref, cos_ref, sin_ref,
                   qk_ref, vm_ref, om_ref, gates_ref, qa_ref, ka_ref, va_ref):
    h = _rmsnorm(x_ref[...], g_ref[...], EPS).astype(BF16)

    def seg(bounds):
        return jnp.dot(h, w_ref[:, bounds[0]:bounds[1]], preferred_element_type=F32)

    qk_ref[...] = seg(_SEG_QK).astype(BF16)
    vm_ref[...] = seg(_SEG_VM).astype(BF16)
    om_ref[...] = seg(_SEG_OM).astype(BF16)
    va_ref[...] = seg(_SEG_VA).astype(BF16)
    gates_ref[...] = jnp.dot(h, wg_ref[...], preferred_element_type=F32) + bg_ref[...]

    cos = cos_ref[...]
    sin_signed = sin_ref[...]
    lane = lax.broadcasted_iota(jnp.int32, (1, LANES), 1)
    first_half = (lane % A_QK_DIM) < (A_QK_DIM // 2)
    q = seg(_SEG_QA)
    k = seg(_SEG_KA)
    scale = A_QK_DIM ** -0.5 * math.log2(math.e)
    for j in range(A_WIDTH // LANES):
        sl = slice(j * LANES, (j + 1) * LANES)
        qa_ref[:, sl] = (_rope_tile(q[:, sl], cos, sin_signed, first_half) * scale).astype(BF16)
        ka_ref[:, sl] = _rope_tile(k[:, sl], cos, sin_signed, first_half).astype(BF16)


def _inproj(x, g, w_main, w_gate, b_gate, cos_t, sin_t, seq):
    rows = x.shape[0]
    tm = ROW_TILE
    pos_blocks = seq // tm
    row = lambda i: (i, 0)
    fixed = lambda i: (0, 0)
    pos = lambda i: (i % pos_blocks, 0)
    out_shapes = (
        jax.ShapeDtypeStruct((rows, 2 * M_WIDTH), BF16),
        jax.ShapeDtypeStruct((rows, M_WIDTH), BF16),
        jax.ShapeDtypeStruct((rows, M_WIDTH), BF16),
        jax.ShapeDtypeStruct((rows, LANES), F32),
        jax.ShapeDtypeStruct((rows, A_WIDTH), BF16),
        jax.ShapeDtypeStruct((rows, A_WIDTH), BF16),
        jax.ShapeDtypeStruct((rows, A_WIDTH), BF16),
    )
    out_specs = (
        pl.BlockSpec((tm, 2 * M_WIDTH), row),
        pl.BlockSpec((tm, M_WIDTH), row),
        pl.BlockSpec((tm, M_WIDTH), row),
        pl.BlockSpec((tm, LANES), row),
        pl.BlockSpec((tm, A_WIDTH), row),
        pl.BlockSpec((tm, A_WIDTH), row),
        pl.BlockSpec((tm, A_WIDTH), row),
    )
    est = (2 * (tm * D_MODEL * 4 + tm * MAIN_WIDTH * 2 + tm * LANES * 4 + 2 * tm * LANES * 4)
           + 2 * (D_MODEL * (MAIN_WIDTH + LANES) * 2) + 3 * tm * 2 * M_WIDTH * 4)
    return pl.pallas_call(
        _inproj_kernel,
        grid=(rows // tm,),
        in_specs=[
            pl.BlockSpec((tm, D_MODEL), row),
            pl.BlockSpec((1, D_MODEL), fixed),
            pl.BlockSpec((D_MODEL, MAIN_WIDTH), fixed),
            pl.BlockSpec((D_MODEL, LANES), fixed),
            pl.BlockSpec((1, LANES), fixed),
            pl.BlockSpec((tm, LANES), pos),
            pl.BlockSpec((tm, LANES), pos),
        ],
        out_specs=out_specs,
        out_shape=out_shapes,
        compiler_params=pltpu.CompilerParams(
            dimension_semantics=("arbitrary",), vmem_limit_bytes=_vmem_limit(est)),
        name="inproj",
    )(x, g, w_main, w_gate, b_gate, cos_t, sin_t)


def _mlstm_kernel(qk_ref, v_ref, o_ref, gate_ref, cw_ref, cb_ref, gh_ref, out_ref,
                  pad_scr, ct_scr, n_scr, m_scr, *, seq, chunk):
    L = chunk
    pad_scr[0:SUBLANES, :] = jnp.zeros((SUBLANES, 2 * M_WIDTH), F32)

    def fill(i, carry):
        r = pl.multiple_of(i * CONV_FILL_ROWS, CONV_FILL_ROWS)
        dst = pl.multiple_of(r + SUBLANES, SUBLANES)
        pad_scr[pl.ds(dst, CONV_FILL_ROWS), :] = qk_ref[pl.ds(r, CONV_FILL_ROWS), :].astype(F32)
        return carry

    lax.fori_loop(0, seq // CONV_FILL_ROWS, fill, 0)
    ct_scr[...] = jnp.zeros(ct_scr.shape, F32)
    n_scr[...] = jnp.zeros(n_scr.shape, F32)
    m_scr[...] = jnp.zeros(m_scr.shape, F32)

    def body(c, carry):
        r0 = pl.multiple_of(c * L, L)
        u = pad_scr[pl.ds(r0, L + SUBLANES), :]
        off = SUBLANES - (CONV_K - 1)
        acc = u[off:off + L] * cw_ref[0:1, :]
        for j in range(1, CONV_K):
            acc = acc + u[off + j:off + j + L] * cw_ref[j:j + 1, :]
        acc = acc + cb_ref[...]
        qk = acc * jax.nn.sigmoid(acc)

        gates = gate_ref[pl.ds(r0, L), :]
        log_f = _log_sigmoid(gates)
        row = lax.broadcasted_iota(jnp.int32, (L, L), 0)
        col = lax.broadcasted_iota(jnp.int32, (L, L), 1)
        tril = row >= col
        bcum = jnp.dot(tril.astype(F32), log_f, precision=lax.Precision.HIGHEST,
                       preferred_element_type=F32)
        gates_t = gates.T
        bcum_t = bcum.T

        for h in range(M_HEADS):
            hs = slice(h * HEAD_DIM, (h + 1) * HEAD_DIM)
            ks = slice(M_WIDTH + h * HEAD_DIM, M_WIDTH + (h + 1) * HEAD_DIM)
            b_col = bcum[:, M_HEADS + h:M_HEADS + h + 1]
            i_col = gates[:, h:h + 1]
            b_row = bcum_t[M_HEADS + h:M_HEADS + h + 1, :]
            i_row = gates_t[h:h + 1, :]
            m_prev = m_scr[h:h + 1, 0:1]

            dlog = jnp.where(tril, b_col - b_row + i_row, -jnp.inf)
            inter = b_col + m_prev
            m_t = jnp.maximum(inter, jnp.max(dlog, axis=-1, keepdims=True))
            w_intra = jnp.exp(dlog - m_t)
            w_inter = jnp.exp(inter - m_t)

            q = qk[:, hs]
            k = qk[:, ks] * (HEAD_DIM ** -0.5)
            qb = q.astype(BF16)
            kb = k.astype(BF16)
            vb = v_ref[pl.ds(r0, L), hs]
            sc = lax.dot_general(qb, kb, _NT, preferred_element_type=F32) * w_intra
            ct = ct_scr[h]
            n_row = n_scr[h:h + 1, :]
            num = (jnp.dot(sc.astype(BF16), vb, preferred_element_type=F32)
                   + w_inter * jnp.dot(qb, ct.astype(BF16), preferred_element_type=F32))
            den = (jnp.sum(sc, axis=-1, keepdims=True)
                   + w_inter * jnp.sum(q * n_row, axis=-1, keepdims=True))
            hh = num / jnp.maximum(jnp.abs(den), jnp.exp(-m_t))
            hn = _rmsnorm(hh, gh_ref[:, hs], HEAD_EPS)
            og = o_ref[pl.ds(r0, L), hs].astype(F32)
            out_ref[pl.ds(r0, L), hs] = (jax.nn.sigmoid(og) * hn).astype(BF16)

            b_last = b_col[L - 1:L, :]
            g_col = b_last - b_col + i_col
            m_new = jnp.maximum(b_last + m_prev, jnp.max(g_col, axis=0, keepdims=True))
            ws = jnp.exp(g_col - m_new)
            decay = jnp.exp(b_last + m_prev - m_new)
            kw = k * ws
            ct_scr[h] = decay * ct + jnp.dot(kw.T.astype(BF16), vb, preferred_element_type=F32)
            n_scr[h:h + 1, :] = decay * n_row + jnp.sum(kw, axis=0, keepdims=True)
            m_scr[h:h + 1, :] = jnp.broadcast_to(m_new, (1, LANES))
        return carry

    lax.fori_loop(0, seq // L, body, 0)


def _mlstm(qk, v, o, gates, conv_w, conv_b, g_head, batch, seq):
    per_b = lambda b: (b, 0)
    fixed = lambda b: (0, 0)
    est = (2 * seq * (2 * M_WIDTH * 2 + 3 * M_WIDTH * 2 + LANES * 4)
           + (seq + SUBLANES) * 2 * M_WIDTH * 4 + 8 * MLSTM_CHUNK * 2 * M_WIDTH * 4)
    return pl.pallas_call(
        functools.partial(_mlstm_kernel, seq=seq, chunk=MLSTM_CHUNK),
        grid=(batch,),
        in_specs=[
            pl.BlockSpec((seq, 2 * M_WIDTH), per_b),
            pl.BlockSpec((seq, M_WIDTH), per_b),
            pl.BlockSpec((seq, M_WIDTH), per_b),
            pl.BlockSpec((seq, LANES), per_b),
            pl.BlockSpec((CONV_K, 2 * M_WIDTH), fixed),
            pl.BlockSpec((1, 2 * M_WIDTH), fixed),
            pl.BlockSpec((1, M_WIDTH), fixed),
        ],
        out_specs=pl.BlockSpec((seq, M_WIDTH), per_b),
        out_shape=jax.ShapeDtypeStruct((batch * seq, M_WIDTH), BF16),
        scratch_shapes=[
            pltpu.VMEM((seq + SUBLANES, 2 * M_WIDTH), F32),
            pltpu.VMEM((M_HEADS, HEAD_DIM, HEAD_DIM), F32),
            pltpu.VMEM((SUBLANES, LANES), F32),
            pltpu.VMEM((SUBLANES, LANES), F32),
        ],
        compiler_params=pltpu.CompilerParams(
            dimension_semantics=("arbitrary",), vmem_limit_bytes=_vmem_limit(est)),
        name="mlstm",
    )(qk, v, o, gates, conv_w, conv_b, g_head)


def _pv(p, v):
    half = p.shape[0] // 2
    return jnp.concatenate(
        [jnp.dot(p[:half], v, preferred_element_type=F32),
         jnp.dot(p[half:], v, preferred_element_type=F32)], axis=0)


def _attn_kernel(lam_ref, q_ref, k_ref, v_ref, gh_ref, o_ref, *, seq, tq, lam_init):
    lane = lax.broadcasted_iota(jnp.int32, (1, LANES), 1)
    map_masks = tuple(((lane // A_QK_DIM) == c).astype(F32) for c in range(2))
    row = lax.broadcasted_iota(jnp.int32, (tq, tq), 0)
    col = lax.broadcasted_iota(jnp.int32, (tq, tq), 1)
    causal = col <= row
    lp = lam_ref[...]
    lam = (jnp.exp(jnp.sum(lp[0:1] * lp[1:2], axis=-1, keepdims=True))
           - jnp.exp(jnp.sum(lp[2:3] * lp[3:4], axis=-1, keepdims=True)) + lam_init)
    gh = gh_ref[...]
    ones_tile = (lax.broadcasted_iota(jnp.int32, (seq, HEAD_DIM), 1) == 0).astype(BF16)
    v_aug = jnp.concatenate([v_ref[...], ones_tile], axis=1)

    for qi in range(seq // tq):
        r0 = qi * tq
        qf = q_ref[r0:r0 + tq, :].astype(F32)
        k_diag = k_ref[r0:r0 + tq, :]
        maps = []
        for c in range(2):
            qc = (qf * map_masks[c]).astype(BF16)
            s_d = jnp.where(causal,
                            lax.dot_general(qc, k_diag, _NT, preferred_element_type=F32),
                            -jnp.inf)
            m = jnp.max(s_d, axis=-1, keepdims=True)
            if qi > 0:
                s_p = lax.dot_general(qc, k_ref[0:r0, :], _NT, preferred_element_type=F32)
                m = jnp.maximum(m, jnp.max(s_p, axis=-1, keepdims=True))
            acc = _pv(jnp.exp2(s_d - m).astype(BF16), v_aug[r0:r0 + tq])
            if qi > 0:
                acc = acc + _pv(jnp.exp2(s_p - m).astype(BF16), v_aug[0:r0])
            maps.append(acc[:, :HEAD_DIM] / acc[:, HEAD_DIM:HEAD_DIM + 1])
        o = maps[0] - lam * maps[1]
        o_ref[r0:r0 + tq, :] = (_rmsnorm(o, gh, HEAD_EPS) * (1.0 - lam_init)).astype(BF16)


def _diff_attention(lam_pack, q, k, v, g_head, lam_init, batch, seq):
    per_head = lambda b, h: (b, h)
    est = 2 * 4 * seq * HEAD_DIM * 2 + 8 * ATTN_TQ * seq * 4
    return pl.pallas_call(
        functools.partial(_attn_kernel, seq=seq, tq=ATTN_TQ, lam_init=lam_init),
        grid=(batch, A_HEADS),
        in_specs=[
            pl.BlockSpec((SUBLANES, LANES), lambda b, h: (0, 0)),
            pl.BlockSpec((seq, HEAD_DIM), per_head),
            pl.BlockSpec((seq, HEAD_DIM), per_head),
            pl.BlockSpec((seq, HEAD_DIM), per_head),
            pl.BlockSpec((1, HEAD_DIM), lambda b, h: (0, h)),
        ],
        out_specs=pl.BlockSpec((seq, HEAD_DIM), per_head),
        out_shape=jax.ShapeDtypeStruct((batch * seq, A_WIDTH), BF16),
        compiler_params=pltpu.CompilerParams(
            dimension_semantics=("arbitrary", "arbitrary"), vmem_limit_bytes=_vmem_limit(est)),
        name="diffattn",
    )(lam_pack, q, k, v, g_head)


def _ffn_kernel(x_ref, m_ref, a_ref, wo_ref, g_ref, wg_ref, wu_ref, wd_ref, gf_ref, o_ref,
                *, final_norm):
    mix = (jnp.dot(m_ref[...], wo_ref[0:M_WIDTH, :], preferred_element_type=F32)
           + jnp.dot(a_ref[...], wo_ref[M_WIDTH:M_WIDTH + A_WIDTH, :],
                     preferred_element_type=F32))
    x1 = x_ref[...] + mix
    h2 = _rmsnorm(x1, g_ref[...], EPS).astype(BF16)
    acc = x1
    for lo, hi in FFN_CHUNKS:
        gate = jnp.dot(h2, wg_ref[:, lo:hi], preferred_element_type=F32)
        up = jnp.dot(h2, wu_ref[:, lo:hi], preferred_element_type=F32)
        act = (gate * jax.nn.sigmoid(gate) * up).astype(BF16)
        acc = acc + jnp.dot(act, wd_ref[lo:hi, :], preferred_element_type=F32)
    if final_norm:
        acc = _rmsnorm(acc, gf_ref[...], EPS)
    o_ref[...] = acc


def _ffn(x, m_out, a_out, w_out, g_ffn, w_gate, w_up, w_down, g_final, final_norm):
    rows = x.shape[0]
    tm = ROW_TILE
    row = lambda i: (i, 0)
    fixed = lambda i: (0, 0)
    resident = functools.partial(pl.BlockSpec, index_map=fixed, pipeline_mode=pl.Buffered(1))
    weights = 2 * (D_MODEL * D_MODEL + 3 * D_MODEL * FFN_HIDDEN)
    est = (weights + 2 * (2 * tm * D_MODEL * 4 + 2 * tm * M_WIDTH * 2)
           + 4 * tm * D_MODEL * 4 + 4 * tm * 768 * 4)
    return pl.pallas_call(
        functools.partial(_ffn_kernel, final_norm=final_norm),
        grid=(rows // tm,),
        in_specs=[
            pl.BlockSpec((tm, D_MODEL), row),
            pl.BlockSpec((tm, M_WIDTH), row),
            pl.BlockSpec((tm, A_WIDTH), row),
            resident((D_MODEL, D_MODEL)),
            pl.BlockSpec((1, D_MODEL), fixed),
            resident((D_MODEL, FFN_HIDDEN)),
            resident((D_MODEL, FFN_HIDDEN)),
            resident((FFN_HIDDEN, D_MODEL)),
            pl.BlockSpec((1, D_MODEL), fixed),
        ],
        out_specs=pl.BlockSpec((tm, D_MODEL), row),
        out_shape=jax.ShapeDtypeStruct((rows, D_MODEL), F32),
        compiler_params=pltpu.CompilerParams(
            dimension_semantics=("arbitrary",), vmem_limit_bytes=_vmem_limit(est)),
        name="ffn",
    )(x, m_out, a_out, w_out, g_ffn, w_gate, w_up, w_down, g_final)


def _rope_lane_tables(seq):
    pos = jnp.arange(seq, dtype=F32)
    inv = ROPE_THETA ** (-jnp.arange(0, A_QK_DIM, 2, dtype=F32) / A_QK_DIM)
    ang = pos[:, None] * inv[None, :]
    cos, sin = jnp.cos(ang), jnp.sin(ang)
    reps = LANES // A_QK_DIM
    cos_t = jnp.tile(cos, (1, 2 * reps))
    sin_t = jnp.tile(jnp.concatenate([-sin, sin], axis=1), (1, reps))
    return cos_t, sin_t


def _pad_lanes(a):
    return jnp.pad(a, ((0, 0), (0, LANES - a.shape[-1])))


def kernel(x, g_mix, w_in, conv_w, conv_b, b_gates, g_mlstm_head, lam_q1, lam_k1, lam_q2,
           lam_k2, g_diff_head, w_out, g_ffn, w_gate, w_up, w_down, g_final):
    batch, seq, d_model = x.shape
    depth = w_in.shape[0]
    assert d_model == D_MODEL and seq % ROW_TILE == 0 and seq % MLSTM_CHUNK == 0
    assert seq % ATTN_TQ == 0
    cos_t, sin_t = _rope_lane_tables(seq)
    gate_lo = 4 * M_WIDTH
    gate_hi = gate_lo + N_GATES

    xf = x.reshape(batch * seq, d_model)
    for l in range(depth):
        w_l = w_in[l]
        w_main = jnp.concatenate([w_l[:, :gate_lo], w_l[:, gate_hi:]], axis=1).astype(BF16)
        w_g = _pad_lanes(w_l[:, gate_lo:gate_hi]).astype(BF16)
        b_g = _pad_lanes(b_gates[l][None, :])
        qk_m, v_m, o_m, gates, q_a, k_a, v_a = _inproj(
            xf, g_mix[l][None, :], w_main, w_g, b_g, cos_t, sin_t, seq)

        m_out = _mlstm(qk_m, v_m, o_m, gates, conv_w[l], conv_b[l][None, :],
                       g_mlstm_head[l][None, :], batch, seq)

        lam_init = 0.8 - 0.6 * math.exp(-0.3 * l)
        lam_pack = jnp.pad(jnp.stack([lam_q1[l], lam_k1[l], lam_q2[l], lam_k2[l]]),
                           ((0, SUBLANES - 4), (0, LANES - A_QK_DIM)))
        a_out = _diff_attention(lam_pack, q_a, k_a, v_a, g_diff_head[l][None, :], lam_init,
                                batch, seq)

        xf = _ffn(xf, m_out, a_out, w_out[l].astype(BF16), g_ffn[l][None, :],
                  w_gate[l].astype(BF16), w_up[l].astype(BF16), w_down[l].astype(BF16),
                  g_final[None, :], final_norm=(l == depth - 1))
    return xf.reshape(batch, seq, d_model)
```

```python
import functools
import math

import jax
import jax.numpy as jnp
from jax import lax
from jax.experimental import pallas as pl
from jax.experimental.pallas import tpu as pltpu

D_MODEL = 1024
M_WIDTH = 512
A_WIDTH = 512
M_HEADS = 4
A_HEADS = 4
HEAD_DIM = 128
A_QK_DIM = 64
CONV_K = 4
ROPE_THETA = 10000.0
FFN_HIDDEN = 2816
EPS = 1e-6
HEAD_EPS = 1e-5
N_GATES = 2 * M_HEADS

LANES = 128
SUBLANES = 8
V7X_VMEM_BYTES = 64 * 1024 * 1024

F32 = jnp.float32
BF16 = jnp.bfloat16

ROW_TILE = 512
MLSTM_CHUNK = 512
CONV_TILE = 512
ATTN_TQ = 512
FFN_CHUNKS = ((0, 768), (768, 1536), (1536, 2304), (2304, 2816))

_SEG_QK = (0, 2 * M_WIDTH)
_SEG_VM = (_SEG_QK[1], _SEG_QK[1] + M_WIDTH)
_SEG_OM = (_SEG_VM[1], _SEG_VM[1] + M_WIDTH)
_SEG_QA = (_SEG_OM[1], _SEG_OM[1] + A_WIDTH)
_SEG_KA = (_SEG_QA[1], _SEG_QA[1] + A_WIDTH)
_SEG_VA = (_SEG_KA[1], _SEG_KA[1] + A_WIDTH)
MAIN_WIDTH = _SEG_VA[1]

_NT = (((1,), (1,)), ((), ()))


def _vmem_limit(estimate_bytes):
    return int(min(estimate_bytes * 5 // 4 + (8 << 20), V7X_VMEM_BYTES - (6 << 20)))


def _rmsnorm(x, g, eps):
    return x * lax.rsqrt(jnp.mean(x * x, axis=-1, keepdims=True) + eps) * g


def _log_sigmoid(x):
    return jnp.minimum(x, 0.0) - jnp.log1p(jnp.exp(-jnp.abs(x)))


def _rope_tile(t, cos, sin_signed, first_half):
    partner = jnp.where(first_half, pltpu.roll(t, LANES - 32, axis=1), pltpu.roll(t, 32, axis=1))
    return t * cos + partner * sin_signed


def _inproj_kernel(x_ref, g_ref, w_ref, wg_ref, bg_ref, cos_ref, sin_ref,
                   qk_ref, vm_ref, om_ref, gates_ref, qa_ref, ka_ref, va_ref):
    h = _rmsnorm(x_ref[...], g_ref[...], EPS).astype(BF16)

    def seg(bounds):
        return jnp.dot(h, w_ref[:, bounds[0]:bounds[1]], preferred_element_type=F32)

    qk_ref[...] = seg(_SEG_QK).astype(BF16)
    vm_ref[...] = seg(_SEG_VM).astype(BF16)
    om_ref[...] = seg(_SEG_OM).astype(BF16)
    va_ref[...] = seg(_SEG_VA).astype(BF16)
    gates_ref[...] = jnp.dot(h, wg_ref[...], preferred_element_type=F32) + bg_ref[...]

    cos = cos_ref[...]
    sin_signed = sin_ref[...]
    lane = lax.broadcasted_iota(jnp.int32, (1, LANES), 1)
    first_half = (lane % A_QK_DIM) < (A_QK_DIM // 2)
    q = seg(_SEG_QA)
    k = seg(_SEG_KA)
    scale = A_QK_DIM ** -0.5 * math.log2(math.e)
    for j in range(A_WIDTH // LANES):
        sl = slice(j * LANES, (j + 1) * LANES)
        qa_ref[:, sl] = (_rope_tile(q[:, sl], cos, sin_signed, first_half) * scale).astype(BF16)
        ka_ref[:, sl] = _rope_tile(k[:, sl], cos, sin_signed, first_half).astype(BF16)


def _inproj(x, g, w_main, w_gate, b_gate, cos_t, sin_t, seq):
    rows = x.shape[0]
    tm = ROW_TILE
    pos_blocks = seq // tm
    row = lambda i: (i, 0)
    fixed = lambda i: (0, 0)
    pos = lambda i: (i % pos_blocks, 0)
    out_shapes = (
        jax.ShapeDtypeStruct((rows, 2 * M_WIDTH), BF16),
        jax.ShapeDtypeStruct((rows, M_WIDTH), BF16),
        jax.ShapeDtypeStruct((rows, M_WIDTH), BF16),
        jax.ShapeDtypeStruct((rows, LANES), F32),
        jax.ShapeDtypeStruct((rows, A_WIDTH), BF16),
        jax.ShapeDtypeStruct((rows, A_WIDTH), BF16),
        jax.ShapeDtypeStruct((rows, A_WIDTH), BF16),
    )
    out_specs = (
        pl.BlockSpec((tm, 2 * M_WIDTH), row),
        pl.BlockSpec((tm, M_WIDTH), row),
        pl.BlockSpec((tm, M_WIDTH), row),
        pl.BlockSpec((tm, LANES), row),
        pl.BlockSpec((tm, A_WIDTH), row),
        pl.BlockSpec((tm, A_WIDTH), row),
        pl.BlockSpec((tm, A_WIDTH), row),
    )
    est = (2 * (tm * D_MODEL * 4 + tm * MAIN_WIDTH * 2 + tm * LANES * 4 + 2 * tm * LANES * 4)
           + 2 * (D_MODEL * (MAIN_WIDTH + LANES) * 2) + 3 * tm * 2 * M_WIDTH * 4)
    return pl.pallas_call(
        _inproj_kernel,
        grid=(rows // tm,),
        in_specs=[
            pl.BlockSpec((tm, D_MODEL), row),
            pl.BlockSpec((1, D_MODEL), fixed),
            pl.BlockSpec((D_MODEL, MAIN_WIDTH), fixed),
            pl.BlockSpec((D_MODEL, LANES), fixed),
            pl.BlockSpec((1, LANES), fixed),
            pl.BlockSpec((tm, LANES), pos),
            pl.BlockSpec((tm, LANES), pos),
        ],
        out_specs=out_specs,
        out_shape=out_shapes,
        compiler_params=pltpu.CompilerParams(
            dimension_semantics=("arbitrary",), vmem_limit_bytes=_vmem_limit(est)),
        name="inproj",
    )(x, g, w_main, w_gate, b_gate, cos_t, sin_t)


def _lane_scan(x, lane, combine, identity):
    shift = 1
    while shift < LANES:
        x = combine(x, jnp.where(lane >= shift, pltpu.roll(x, shift, axis=1), identity))
        shift *= 2
    return x


def _group_scan(x, pos_in_group, group, combine, identity):
    shift = 1
    while shift < group:
        moved = pltpu.roll(x, SUBLANES * shift, axis=0)
        x = combine(x, jnp.where(pos_in_group >= shift, moved, identity))
        shift *= 2
    return x


def _mlstm_gate_terms(gate_ref, seq, chunk):
    n_tiles = seq // LANES
    per_chunk = chunk // LANES
    stacked = jnp.concatenate(
        [gate_ref[v * LANES:(v + 1) * LANES, :].T[0:SUBLANES, :] for v in range(n_tiles)], axis=0)
    shape = (n_tiles * SUBLANES, LANES)
    lane = lax.broadcasted_iota(jnp.int32, shape, 1)
    row = lax.broadcasted_iota(jnp.int32, shape, 0)
    is_i = (row % SUBLANES) < M_HEADS
    pos = (row // SUBLANES) % per_chunk

    log_f = jnp.where(is_i, 0.0, _log_sigmoid(stacked))
    in_tile = _lane_scan(log_f, lane, jnp.add, 0.0)
    tile_sum = jnp.broadcast_to(in_tile[:, LANES - 1:LANES], shape)
    bcum = in_tile + (_group_scan(tile_sum, pos, per_chunk, jnp.add, 0.0) - tile_sum)
    a = stacked - pltpu.roll(bcum, shape[0] - M_HEADS, axis=0)
    in_tile_max = _lane_scan(a, lane, jnp.maximum, -jnp.inf)
    tile_max = jnp.broadcast_to(in_tile_max[:, LANES - 1:LANES], shape)
    incl = _group_scan(tile_max, pos, per_chunk, jnp.maximum, -jnp.inf)
    before = jnp.where(pos >= 1, pltpu.roll(incl, SUBLANES, axis=0), -jnp.inf)
    cmax = jnp.maximum(in_tile_max, before)
    packed = jnp.where(is_i, cmax, bcum)
    pad_rows = jnp.zeros((LANES - SUBLANES, LANES), F32)
    cols = [jnp.concatenate([packed[v * SUBLANES:(v + 1) * SUBLANES, :], pad_rows], axis=0).T
            for v in range(n_tiles)]
    return a, cols


def _mlstm_kernel(qk_ref, v_ref, o_ref, gate_ref, cw_ref, cb_ref, gh_ref, out_ref,
                  q_scr, k_scr, *, seq, chunk):
    L = chunk
    per_chunk = L // LANES

    off = SUBLANES - (CONV_K - 1)
    for t in range(seq // CONV_TILE):
        r = t * CONV_TILE
        cur = qk_ref[r:r + CONV_TILE, :].astype(F32)
        if t == 0:
            prev = jnp.zeros((SUBLANES, 2 * M_WIDTH), F32)
        else:
            prev = qk_ref[r - 2 * SUBLANES:r, :].astype(F32)[SUBLANES:2 * SUBLANES]
        ext = jnp.concatenate([prev, cur], axis=0)
        acc = cur * cw_ref[CONV_K - 1:CONV_K, :] + cb_ref[...]
        for j in range(CONV_K - 2, -1, -1):
            acc = acc + ext[off + j:off + j + CONV_TILE] * cw_ref[j:j + 1, :]
        qk = acc * jax.nn.sigmoid(acc)
        q_scr[r:r + CONV_TILE, :] = qk[:, :M_WIDTH].astype(BF16)
        k_scr[r:r + CONV_TILE, :] = (qk[:, M_WIDTH:] * (HEAD_DIM ** -0.5)).astype(BF16)

    a_all, cols = _mlstm_gate_terms(gate_ref, seq, L)
    tril = (lax.broadcasted_iota(jnp.int32, (L, L), 0)
            >= lax.broadcasted_iota(jnp.int32, (L, L), 1))
    ones_tile = (lax.broadcasted_iota(jnp.int32, (L, HEAD_DIM), 1) == 0).astype(BF16)

    for h in range(M_HEADS):
        hs = slice(h * HEAD_DIM, (h + 1) * HEAD_DIM)
        gh = gh_ref[:, hs]
        ct = None
        m_prev = jnp.zeros((1, 1), F32)
        for c in range(seq // L):
            r0 = c * L
            tiles = range(c * per_chunk, (c + 1) * per_chunk)
            a_row = jnp.concatenate(
                [a_all[v * SUBLANES + h:v * SUBLANES + h + 1, :] for v in tiles], axis=1)
            cmax_col = jnp.concatenate([cols[v][:, h:h + 1] for v in tiles], axis=0)
            b_col = jnp.concatenate(
                [cols[v][:, M_HEADS + h:M_HEADS + h + 1] for v in tiles], axis=0)

            big_m = jnp.maximum(m_prev, cmax_col)
            w_intra = jnp.exp(jnp.where(tril, a_row - big_m, -jnp.inf))
            qb = q_scr[r0:r0 + L, hs]
            kb = k_scr[r0:r0 + L, hs]
            v_aug = jnp.concatenate([v_ref[r0:r0 + L, hs], ones_tile], axis=1)
            sc = lax.dot_general(qb, kb, _NT, preferred_element_type=F32) * w_intra
            both = _pv(sc.astype(BF16), v_aug)
            if ct is not None:
                w_inter = jnp.exp(m_prev - big_m)
                both = both + w_inter * jnp.dot(qb, ct.astype(BF16), preferred_element_type=F32)
            num = both[:, :HEAD_DIM]
            den = both[:, HEAD_DIM:HEAD_DIM + 1]
            hh = num / jnp.maximum(jnp.abs(den), jnp.exp(-(b_col + big_m)))
            hn = _rmsnorm(hh, gh, HEAD_EPS)
            og = o_ref[r0:r0 + L, hs].astype(F32)
            out_ref[r0:r0 + L, hs] = (jax.nn.sigmoid(og) * hn).astype(BF16)

            if c + 1 < seq // L:
                m_last = big_m[L - 1:L, :]
                ws_row = jnp.exp(a_row - m_last)
                kw_t = (kb.astype(F32).T * ws_row).astype(BF16)
                update = jnp.dot(kw_t, v_aug, preferred_element_type=F32)
                ct = update if ct is None else jnp.exp(m_prev - m_last) * ct + update
                m_prev = b_col[L - 1:L, :] + m_last


def _mlstm(qk, v, o, gates, conv_w, conv_b, g_head, batch, seq):
    per_b = lambda b: (b, 0)
    fixed = lambda b: (0, 0)
    est = (2 * seq * (2 * M_WIDTH * 2 + 3 * M_WIDTH * 2 + LANES * 4) + 2 * seq * M_WIDTH * 2
           + 4 * CONV_TILE * 2 * M_WIDTH * 4 + 16 * MLSTM_CHUNK * MLSTM_CHUNK * 4)
    return pl.pallas_call(
        functools.partial(_mlstm_kernel, seq=seq, chunk=MLSTM_CHUNK),
        grid=(batch,),
        in_specs=[
            pl.BlockSpec((seq, 2 * M_WIDTH), per_b),
            pl.BlockSpec((seq, M_WIDTH), per_b),
            pl.BlockSpec((seq, M_WIDTH), per_b),
            pl.BlockSpec((seq, LANES), per_b),
            pl.BlockSpec((CONV_K, 2 * M_WIDTH), fixed),
            pl.BlockSpec((1, 2 * M_WIDTH), fixed),
            pl.BlockSpec((1, M_WIDTH), fixed),
        ],
        out_specs=pl.BlockSpec((seq, M_WIDTH), per_b),
        out_shape=jax.ShapeDtypeStruct((batch * seq, M_WIDTH), BF16),
        scratch_shapes=[
            pltpu.VMEM((seq, M_WIDTH), BF16),
            pltpu.VMEM((seq, M_WIDTH), BF16),
        ],
        compiler_params=pltpu.CompilerParams(
            dimension_semantics=("arbitrary",), vmem_limit_bytes=_vmem_limit(est)),
        name="mlstm",
    )(qk, v, o, gates, conv_w, conv_b, g_head)


def _pv(p, v):
    half = p.shape[0] // 2
    return jnp.concatenate(
        [jnp.dot(p[:half], v, preferred_element_type=F32),
         jnp.dot(p[half:], v, preferred_element_type=F32)], axis=0)


def _attn_kernel(lam_ref, q_ref, k_ref, v_ref, gh_ref, o_ref, *, seq, tq, lam_init):
    lane = lax.broadcasted_iota(jnp.int32, (1, LANES), 1)
    map_masks = tuple(((lane // A_QK_DIM) == c).astype(F32) for c in range(2))
    row = lax.broadcasted_iota(jnp.int32, (tq, tq), 0)
    col = lax.broadcasted_iota(jnp.int32, (tq, tq), 1)
    causal = col <= row
    lp = lam_ref[...]
    lam = (jnp.exp(jnp.sum(lp[0:1] * lp[1:2], axis=-1, keepdims=True))
           - jnp.exp(jnp.sum(lp[2:3] * lp[3:4], axis=-1, keepdims=True)) + lam_init)
    gh = gh_ref[...]
    ones_tile = (lax.broadcasted_iota(jnp.int32, (seq, HEAD_DIM), 1) == 0).astype(BF16)
    v_aug = jnp.concatenate([v_ref[...], ones_tile], axis=1)

    for qi in range(seq // tq):
        r0 = qi * tq
        qf = q_ref[r0:r0 + tq, :].astype(F32)
        k_diag = k_ref[r0:r0 + tq, :]
        maps = []
        for c in range(2):
            qc = (qf * map_masks[c]).astype(BF16)
            s_d = jnp.where(causal,
                            lax.dot_general(qc, k_diag, _NT, preferred_element_type=F32),
                            -jnp.inf)
            m = jnp.max(s_d, axis=-1, keepdims=True)
            if qi > 0:
                s_p = lax.dot_general(qc, k_ref[0:r0, :], _NT, preferred_element_type=F32)
                m = jnp.maximum(m, jnp.max(s_p, axis=-1, keepdims=True))
            acc = _pv(jnp.exp2(s_d - m).astype(BF16), v_aug[r0:r0 + tq])
            if qi > 0:
                acc = acc + _pv(jnp.exp2(s_p - m).astype(BF16), v_aug[0:r0])
            maps.append(acc[:, :HEAD_DIM] / acc[:, HEAD_DIM:HEAD_DIM + 1])
        o = maps[0] - lam * maps[1]
        o_ref[r0:r0 + tq, :] = (_rmsnorm(o, gh, HEAD_EPS) * (1.0 - lam_init)).astype(BF16)


def _diff_attention(lam_pack, q, k, v, g_head, lam_init, batch, seq):
    per_head = lambda b, h: (b, h)
    est = 2 * 4 * seq * HEAD_DIM * 2 + 8 * ATTN_TQ * seq * 4
    return pl.pallas_call(
        functools.partial(_attn_kernel, seq=seq, tq=ATTN_TQ, lam_init=lam_init),
        grid=(batch, A_HEADS),
        in_specs=[
            pl.BlockSpec((SUBLANES, LANES), lambda b, h: (0, 0)),
            pl.BlockSpec((seq, HEAD_DIM), per_head),
            pl.BlockSpec((seq, HEAD_DIM), per_head),
            pl.BlockSpec((seq, HEAD_DIM), per_head),
            pl.BlockSpec((1, HEAD_DIM), lambda b, h: (0, h)),
        ],
        out_specs=pl.BlockSpec((seq, HEAD_DIM), per_head),
        out_shape=jax.ShapeDtypeStruct((batch * seq, A_WIDTH), BF16),
        compiler_params=pltpu.CompilerParams(
            dimension_semantics=("arbitrary", "arbitrary"), vmem_limit_bytes=_vmem_limit(est)),
        name="diffattn",
    )(lam_pack, q, k, v, g_head)


def _ffn_kernel(x_ref, m_ref, a_ref, wo_ref, g_ref, wg_ref, wu_ref, wd_ref, gf_ref, o_ref,
                *, final_norm):
    mix = (jnp.dot(m_ref[...], wo_ref[0:M_WIDTH, :], preferred_element_type=F32)
           + jnp.dot(a_ref[...], wo_ref[M_WIDTH:M_WIDTH + A_WIDTH, :],
                     preferred_element_type=F32))
    x1 = x_ref[...] + mix
    h2 = _rmsnorm(x1, g_ref[...], EPS).astype(BF16)
    acc = x1
    for lo, hi in FFN_CHUNKS:
        gate = jnp.dot(h2, wg_ref[:, lo:hi], preferred_element_type=F32)
        up = jnp.dot(h2, wu_ref[:, lo:hi], preferred_element_type=F32)
        act = (gate * jax.nn.sigmoid(gate) * up).astype(BF16)
        acc = acc + jnp.dot(act, wd_ref[lo:hi, :], preferred_element_type=F32)
    if final_norm:
        acc = _rmsnorm(acc, gf_ref[...], EPS)
    o_ref[...] = acc


def _ffn(x, m_out, a_out, w_out, g_ffn, w_gate, w_up, w_down, g_final, final_norm):
    rows = x.shape[0]
    tm = ROW_TILE
    row = lambda i: (i, 0)
    fixed = lambda i: (0, 0)
    resident = functools.partial(pl.BlockSpec, index_map=fixed, pipeline_mode=pl.Buffered(1))
    weights = 2 * (D_MODEL * D_MODEL + 3 * D_MODEL * FFN_HIDDEN)
    est = (weights + 2 * (2 * tm * D_MODEL * 4 + 2 * tm * M_WIDTH * 2)
           + 4 * tm * D_MODEL * 4 + 4 * tm * 768 * 4)
    return pl.pallas_call(
        functools.partial(_ffn_kernel, final_norm=final_norm),
        grid=(rows // tm,),
        in_specs=[
            pl.BlockSpec((tm, D_MODEL), row),
            pl.BlockSpec((tm, M_WIDTH), row),
            pl.BlockSpec((tm, A_WIDTH), row),
            resident((D_MODEL, D_MODEL)),
            pl.BlockSpec((1, D_MODEL), fixed),
            resident((D_MODEL, FFN_HIDDEN)),
            resident((D_MODEL, FFN_HIDDEN)),
            resident((FFN_HIDDEN, D_MODEL)),
            pl.BlockSpec((1, D_MODEL), fixed),
        ],
        out_specs=pl.BlockSpec((tm, D_MODEL), row),
        out_shape=jax.ShapeDtypeStruct((rows, D_MODEL), F32),
        compiler_params=pltpu.CompilerParams(
            dimension_semantics=("arbitrary",), vmem_limit_bytes=_vmem_limit(est)),
        name="ffn",
    )(x, m_out, a_out, w_out, g_ffn, w_gate, w_up, w_down, g_final)


def _rope_lane_tables(seq):
    pos = jnp.arange(seq, dtype=F32)
    inv = ROPE_THETA ** (-jnp.arange(0, A_QK_DIM, 2, dtype=F32) / A_QK_DIM)
    ang = pos[:, None] * inv[None, :]
    cos, sin = jnp.cos(ang), jnp.sin(ang)
    reps = LANES // A_QK_DIM
    cos_t = jnp.tile(cos, (1, 2 * reps))
    sin_t = jnp.tile(jnp.concatenate([-sin, sin], axis=1), (1, reps))
    return cos_t, sin_t


def _pad_lanes(a):
    return jnp.pad(a, ((0, 0), (0, LANES - a.shape[-1])))


def kernel(x, g_mix, w_in, conv_w, conv_b, b_gates, g_mlstm_head, lam_q1, lam_k1, lam_q2,
           lam_k2, g_diff_head, w_out, g_ffn, w_gate, w_up, w_down, g_final):
    batch, seq, d_model = x.shape
    depth = w_in.shape[0]
    assert d_model == D_MODEL and seq % ROW_TILE == 0 and seq % MLSTM_CHUNK == 0
    assert seq % ATTN_TQ == 0
    cos_t, sin_t = _rope_lane_tables(seq)
    gate_lo = 4 * M_WIDTH
    gate_hi = gate_lo + N_GATES

    xf = x.reshape(batch * seq, d_model)
    for l in range(depth):
        w_l = w_in[l]
        w_main = jnp.concatenate([w_l[:, :gate_lo], w_l[:, gate_hi:]], axis=1).astype(BF16)
        w_g = _pad_lanes(w_l[:, gate_lo:gate_hi]).astype(BF16)
        b_g = _pad_lanes(b_gates[l][None, :])
        qk_m, v_m, o_m, gates, q_a, k_a, v_a = _inproj(
            xf, g_mix[l][None, :], w_main, w_g, b_g, cos_t, sin_t, seq)

        m_out = _mlstm(qk_m, v_m, o_m, gates, conv_w[l], conv_b[l][None, :],
                       g_mlstm_head[l][None, :], batch, seq)

        lam_init = 0.8 - 0.6 * math.exp(-0.3 * l)
        lam_pack = jnp.pad(jnp.stack([lam_q1[l], lam_k1[l], lam_q2[l], lam_k2[l]]),
                           ((0, SUBLANES - 4), (0, LANES - A_QK_DIM)))
        a_out = _diff_attention(lam_pack, q_a, k_a, v_a, g_diff_head[l][None, :], lam_init,
                                batch, seq)

        xf = _ffn(xf, m_out, a_out, w_out[l].astype(BF16), g_ffn[l][None, :],
                  w_gate[l].astype(BF16), w_up[l].astype(BF16), w_down[l].astype(BF16),
                  g_final[None, :], final_norm=(l == depth - 1))
    return xf.reshape(batch, seq, d_model)
```

```python
import functools
import math

import jax
import jax.numpy as jnp
from jax import lax
from jax.experimental import pallas as pl
from jax.experimental.pallas import tpu as pltpu

D_MODEL = 1024
M_WIDTH = 512
A_WIDTH = 512
M_HEADS = 4
A_HEADS = 4
HEAD_DIM = 128
A_QK_DIM = 64
CONV_K = 4
ROPE_THETA = 10000.0
FFN_HIDDEN = 2816
EPS = 1e-6
HEAD_EPS = 1e-5
N_GATES = 2 * M_HEADS

LANES = 128
SUBLANES = 8
V7X_VMEM_BYTES = 64 * 1024 * 1024

F32 = jnp.float32
BF16 = jnp.bfloat16

ROW_TILE = 512
MLSTM_CHUNK = 512
CONV_TILE = 512
ATTN_TQ = 512
ATTN_HEADS_PER_STEP = 2
FFN_CHUNKS = ((0, 768), (768, 1536), (1536, 2304), (2304, 2816))

_SEG_QK = (0, 2 * M_WIDTH)
_SEG_VM = (_SEG_QK[1], _SEG_QK[1] + M_WIDTH)
_SEG_OM = (_SEG_VM[1], _SEG_VM[1] + M_WIDTH)
_SEG_QA = (_SEG_OM[1], _SEG_OM[1] + A_WIDTH)
_SEG_KA = (_SEG_QA[1], _SEG_QA[1] + A_WIDTH)
_SEG_VA = (_SEG_KA[1], _SEG_KA[1] + A_WIDTH)
MAIN_WIDTH = _SEG_VA[1]

_LAM_ROWS = 4
_LAM_INIT_ROW = 4

_NT = (((1,), (1,)), ((), ()))


def _vmem_limit(estimate_bytes):
    return int(min(estimate_bytes * 5 // 4 + (8 << 20), V7X_VMEM_BYTES - (6 << 20)))


def _rmsnorm(x, g, eps):
    return x * lax.rsqrt(jnp.mean(x * x, axis=-1, keepdims=True) + eps) * g


def _log_sigmoid(x):
    return jnp.minimum(x, 0.0) - jnp.log1p(jnp.exp(-jnp.abs(x)))


def _rope_tile(t, cos, sin_signed, first_half):
    partner = jnp.where(first_half, pltpu.roll(t, LANES - 32, axis=1), pltpu.roll(t, 32, axis=1))
    return t * cos + partner * sin_signed


def _inproj_kernel(x_ref, g_ref, w_ref, wg_ref, bg_ref, cos_ref, sin_ref,
                   qk_ref, vm_ref, om_ref, gates_ref, qa_ref, ka_ref, va_ref):
    h = _rmsnorm(x_ref[...], g_ref[...], EPS).astype(BF16)

    def seg(bounds):
        return jnp.dot(h, w_ref[:, bounds[0]:bounds[1]], preferred_element_type=F32)

    cos = cos_ref[...]
    sin_signed = sin_ref[...]
    lane = lax.broadcasted_iota(jnp.int32, (1, LANES), 1)
    first_half = (lane % A_QK_DIM) < (A_QK_DIM // 2)
    q = seg(_SEG_QA)
    k = seg(_SEG_KA)
    scale = A_QK_DIM ** -0.5 * math.log2(math.e)
    for j in range(A_WIDTH // LANES):
        sl = slice(j * LANES, (j + 1) * LANES)
        qa_ref[:, sl] = (_rope_tile(q[:, sl], cos, sin_signed, first_half) * scale).astype(BF16)
        ka_ref[:, sl] = _rope_tile(k[:, sl], cos, sin_signed, first_half).astype(BF16)

    gates_ref[...] = jnp.dot(h, wg_ref[...], preferred_element_type=F32) + bg_ref[...]
    qk_ref[...] = seg(_SEG_QK).astype(BF16)
    vm_ref[...] = seg(_SEG_VM).astype(BF16)
    om_ref[...] = seg(_SEG_OM).astype(BF16)
    va_ref[...] = seg(_SEG_VA).astype(BF16)


def _inproj(x, g, w_main, w_gate, b_gate, cos_t, sin_t, seq):
    rows = x.shape[0]
    tm = ROW_TILE
    pos_blocks = seq // tm
    row = lambda i: (i, 0)
    fixed = lambda i: (0, 0)
    pos = lambda i: (i % pos_blocks, 0)
    out_shapes = (
        jax.ShapeDtypeStruct((rows, 2 * M_WIDTH), BF16),
        jax.ShapeDtypeStruct((rows, M_WIDTH), BF16),
        jax.ShapeDtypeStruct((rows, M_WIDTH), BF16),
        jax.ShapeDtypeStruct((rows, LANES), F32),
        jax.ShapeDtypeStruct((rows, A_WIDTH), BF16),
        jax.ShapeDtypeStruct((rows, A_WIDTH), BF16),
        jax.ShapeDtypeStruct((rows, A_WIDTH), BF16),
    )
    out_specs = (
        pl.BlockSpec((tm, 2 * M_WIDTH), row),
        pl.BlockSpec((tm, M_WIDTH), row),
        pl.BlockSpec((tm, M_WIDTH), row),
        pl.BlockSpec((tm, LANES), row),
        pl.BlockSpec((tm, A_WIDTH), row),
        pl.BlockSpec((tm, A_WIDTH), row),
        pl.BlockSpec((tm, A_WIDTH), row),
    )
    est = (2 * (tm * D_MODEL * 4 + tm * MAIN_WIDTH * 2 + tm * LANES * 4 + 2 * tm * LANES * 4)
           + 2 * (D_MODEL * (MAIN_WIDTH + LANES) * 2) + 3 * tm * 2 * M_WIDTH * 4)
    return pl.pallas_call(
        _inproj_kernel,
        grid=(rows // tm,),
        in_specs=[
            pl.BlockSpec((tm, D_MODEL), row),
            pl.BlockSpec((1, D_MODEL), fixed),
            pl.BlockSpec((D_MODEL, MAIN_WIDTH), fixed),
            pl.BlockSpec((D_MODEL, LANES), fixed),
            pl.BlockSpec((1, LANES), fixed),
            pl.BlockSpec((tm, LANES), pos),
            pl.BlockSpec((tm, LANES), pos),
        ],
        out_specs=out_specs,
        out_shape=out_shapes,
        compiler_params=pltpu.CompilerParams(
            dimension_semantics=("arbitrary",), vmem_limit_bytes=_vmem_limit(est)),
        name="inproj",
    )(x, g, w_main, w_gate, b_gate, cos_t, sin_t)


def _lane_scan(x, lane, combine, identity):
    shift = 1
    while shift < LANES:
        x = combine(x, jnp.where(lane >= shift, pltpu.roll(x, shift, axis=1), identity))
        shift *= 2
    return x


def _group_scan(x, pos_in_group, group, combine, identity):
    shift = 1
    while shift < group:
        moved = pltpu.roll(x, SUBLANES * shift, axis=0)
        x = combine(x, jnp.where(pos_in_group >= shift, moved, identity))
        shift *= 2
    return x


def _mlstm_gate_terms(gate_ref, seq, chunk):
    n_tiles = seq // LANES
    per_chunk = chunk // LANES
    stacked = jnp.concatenate(
        [gate_ref[v * LANES:(v + 1) * LANES, :].T[0:SUBLANES, :] for v in range(n_tiles)], axis=0)
    shape = (n_tiles * SUBLANES, LANES)
    lane = lax.broadcasted_iota(jnp.int32, shape, 1)
    row = lax.broadcasted_iota(jnp.int32, shape, 0)
    is_i = (row % SUBLANES) < M_HEADS
    pos = (row // SUBLANES) % per_chunk

    log_f = jnp.where(is_i, 0.0, _log_sigmoid(stacked))
    in_tile = _lane_scan(log_f, lane, jnp.add, 0.0)
    tile_sum = jnp.broadcast_to(in_tile[:, LANES - 1:LANES], shape)
    bcum = in_tile + (_group_scan(tile_sum, pos, per_chunk, jnp.add, 0.0) - tile_sum)
    a = stacked - pltpu.roll(bcum, shape[0] - M_HEADS, axis=0)
    in_tile_max = _lane_scan(a, lane, jnp.maximum, -jnp.inf)
    tile_max = jnp.broadcast_to(in_tile_max[:, LANES - 1:LANES], shape)
    incl = _group_scan(tile_max, pos, per_chunk, jnp.maximum, -jnp.inf)
    before = jnp.where(pos >= 1, pltpu.roll(incl, SUBLANES, axis=0), -jnp.inf)
    cmax = jnp.maximum(in_tile_max, before)
    packed = jnp.where(is_i, cmax, bcum)
    pad_rows = jnp.zeros((LANES - SUBLANES, LANES), F32)
    cols = [jnp.concatenate([packed[v * SUBLANES:(v + 1) * SUBLANES, :], pad_rows], axis=0).T
            for v in range(n_tiles)]
    return a, cols


def _pv(p, v):
    half = p.shape[0] // 2
    return jnp.concatenate(
        [jnp.dot(p[:half], v, preferred_element_type=F32),
         jnp.dot(p[half:], v, preferred_element_type=F32)], axis=0)


def _mlstm_kernel(qk_ref, v_ref, o_ref, gate_ref, cw_ref, cb_ref, gh_ref, out_ref,
                  q_scr, k_scr, *, seq, chunk):
    L = chunk
    per_chunk = L // LANES

    off = SUBLANES - (CONV_K - 1)
    for t in range(seq // CONV_TILE):
        r = t * CONV_TILE
        cur = qk_ref[r:r + CONV_TILE, :].astype(F32)
        if t == 0:
            prev = jnp.zeros((SUBLANES, 2 * M_WIDTH), F32)
        else:
            prev = qk_ref[r - 2 * SUBLANES:r, :].astype(F32)[SUBLANES:2 * SUBLANES]
        ext = jnp.concatenate([prev, cur], axis=0)
        acc = cur * cw_ref[CONV_K - 1:CONV_K, :] + cb_ref[...]
        for j in range(CONV_K - 2, -1, -1):
            acc = acc + ext[off + j:off + j + CONV_TILE] * cw_ref[j:j + 1, :]
        qk = acc * jax.nn.sigmoid(acc)
        q_scr[r:r + CONV_TILE, :] = qk[:, :M_WIDTH].astype(BF16)
        k_scr[r:r + CONV_TILE, :] = (qk[:, M_WIDTH:] * (HEAD_DIM ** -0.5)).astype(BF16)

    a_all, cols = _mlstm_gate_terms(gate_ref, seq, L)
    tril = (lax.broadcasted_iota(jnp.int32, (L, L), 0)
            >= lax.broadcasted_iota(jnp.int32, (L, L), 1))
    ones_tile = (lax.broadcasted_iota(jnp.int32, (L, HEAD_DIM), 1) == 0).astype(BF16)

    for h in range(M_HEADS):
        hs = slice(h * HEAD_DIM, (h + 1) * HEAD_DIM)
        gh = gh_ref[:, hs]
        ct = None
        m_prev = jnp.zeros((1, 1), F32)
        for c in range(seq // L):
            r0 = c * L
            tiles = range(c * per_chunk, (c + 1) * per_chunk)
            a_row = jnp.concatenate(
                [a_all[v * SUBLANES + h:v * SUBLANES + h + 1, :] for v in tiles], axis=1)
            cmax_col = jnp.concatenate([cols[v][:, h:h + 1] for v in tiles], axis=0)
            b_col = jnp.concatenate(
                [cols[v][:, M_HEADS + h:M_HEADS + h + 1] for v in tiles], axis=0)

            big_m = jnp.maximum(m_prev, cmax_col)
            w_intra = jnp.exp(jnp.where(tril, a_row - big_m, -jnp.inf))
            qb = q_scr[r0:r0 + L, hs]
            kb = k_scr[r0:r0 + L, hs]
            v_aug = jnp.concatenate([v_ref[r0:r0 + L, hs], ones_tile], axis=1)
            sc = lax.dot_general(qb, kb, _NT, preferred_element_type=F32) * w_intra
            both = _pv(sc.astype(BF16), v_aug)
            if ct is not None:
                w_inter = jnp.exp(m_prev - big_m)
                both = both + w_inter * jnp.dot(qb, ct.astype(BF16), preferred_element_type=F32)
            num = both[:, :HEAD_DIM]
            den = both[:, HEAD_DIM:HEAD_DIM + 1]
            hh = num / jnp.maximum(jnp.abs(den), jnp.exp(-(b_col + big_m)))
            hn = _rmsnorm(hh, gh, HEAD_EPS)
            og = o_ref[r0:r0 + L, hs].astype(F32)
            out_ref[r0:r0 + L, hs] = (jax.nn.sigmoid(og) * hn).astype(BF16)

            if c + 1 < seq // L:
                m_last = big_m[L - 1:L, :]
                ws_row = jnp.exp(a_row - m_last)
                kw_t = (kb.astype(F32).T * ws_row).astype(BF16)
                update = jnp.dot(kw_t, v_aug, preferred_element_type=F32)
                ct = update if ct is None else jnp.exp(m_prev - m_last) * ct + update
                m_prev = b_col[L - 1:L, :] + m_last


def _mlstm(qk, v, o, gates, conv_w, conv_b, g_head, batch, seq):
    per_b = lambda b: (b, 0)
    fixed = lambda b: (0, 0)
    est = (2 * seq * (2 * M_WIDTH * 2 + 3 * M_WIDTH * 2 + LANES * 4) + 2 * seq * M_WIDTH * 2
           + 4 * CONV_TILE * 2 * M_WIDTH * 4 + 16 * MLSTM_CHUNK * MLSTM_CHUNK * 4)
    return pl.pallas_call(
        functools.partial(_mlstm_kernel, seq=seq, chunk=MLSTM_CHUNK),
        grid=(batch,),
        in_specs=[
            pl.BlockSpec((seq, 2 * M_WIDTH), per_b),
            pl.BlockSpec((seq, M_WIDTH), per_b),
            pl.BlockSpec((seq, M_WIDTH), per_b),
            pl.BlockSpec((seq, LANES), per_b),
            pl.BlockSpec((CONV_K, 2 * M_WIDTH), fixed),
            pl.BlockSpec((1, 2 * M_WIDTH), fixed),
            pl.BlockSpec((1, M_WIDTH), fixed),
        ],
        out_specs=pl.BlockSpec((seq, M_WIDTH), per_b),
        out_shape=jax.ShapeDtypeStruct((batch * seq, M_WIDTH), BF16),
        scratch_shapes=[
            pltpu.VMEM((seq, M_WIDTH), BF16),
            pltpu.VMEM((seq, M_WIDTH), BF16),
        ],
        compiler_params=pltpu.CompilerParams(
            dimension_semantics=("arbitrary",), vmem_limit_bytes=_vmem_limit(est)),
        name="mlstm",
    )(qk, v, o, gates, conv_w, conv_b, g_head)


def _attn_kernel(lam_ref, q_ref, k_ref, v_ref, gh_ref, o_ref, *, seq, tq, heads):
    lane = lax.broadcasted_iota(jnp.int32, (1, LANES), 1)
    map_masks = tuple(((lane // A_QK_DIM) == c).astype(F32) for c in range(2))
    causal = (lax.broadcasted_iota(jnp.int32, (tq, tq), 1)
              <= lax.broadcasted_iota(jnp.int32, (tq, tq), 0))
    lp = lam_ref[...]
    lam_init = lp[_LAM_INIT_ROW:_LAM_INIT_ROW + 1, 0:1]
    lam = (jnp.exp(jnp.sum(lp[0:1] * lp[1:2], axis=-1, keepdims=True))
           - jnp.exp(jnp.sum(lp[2:3] * lp[3:4], axis=-1, keepdims=True)) + lam_init)
    ones_tile = (lax.broadcasted_iota(jnp.int32, (seq, HEAD_DIM), 1) == 0).astype(BF16)

    for j in range(heads):
        js = slice(j * HEAD_DIM, (j + 1) * HEAD_DIM)
        gh = gh_ref[:, js]
        v_aug = jnp.concatenate([v_ref[:, js], ones_tile], axis=1)
        for qi in range(seq // tq):
            r0 = qi * tq
            qf = q_ref[r0:r0 + tq, js].astype(F32)
            k_diag = k_ref[r0:r0 + tq, js]
            maps = []
            for c in range(2):
                qc = (qf * map_masks[c]).astype(BF16)
                s_d = jnp.where(causal,
                                lax.dot_general(qc, k_diag, _NT, preferred_element_type=F32),
                                -jnp.inf)
                m = jnp.max(s_d, axis=-1, keepdims=True)
                if qi > 0:
                    s_p = lax.dot_general(qc, k_ref[0:r0, js], _NT, preferred_element_type=F32)
                    m = jnp.maximum(m, jnp.max(s_p, axis=-1, keepdims=True))
                acc = _pv(jnp.exp2(s_d - m).astype(BF16), v_aug[r0:r0 + tq])
                if qi > 0:
                    acc = acc + _pv(jnp.exp2(s_p - m).astype(BF16), v_aug[0:r0])
                maps.append(acc[:, :HEAD_DIM] / acc[:, HEAD_DIM:HEAD_DIM + 1])
            o = maps[0] - lam * maps[1]
            o_ref[r0:r0 + tq, js] = (_rmsnorm(o, gh, HEAD_EPS) * (1.0 - lam_init)).astype(BF16)


def _diff_attention(lam_pack, q, k, v, g_head, batch, seq):
    heads = ATTN_HEADS_PER_STEP
    width = heads * HEAD_DIM
    per_group = lambda b, g: (b, g)
    est = 2 * 4 * seq * width * 2 + 8 * heads * ATTN_TQ * seq * 4
    return pl.pallas_call(
        functools.partial(_attn_kernel, seq=seq, tq=ATTN_TQ, heads=heads),
        grid=(batch, A_HEADS // heads),
        in_specs=[
            pl.BlockSpec((SUBLANES, LANES), lambda b, g: (0, 0)),
            pl.BlockSpec((seq, width), per_group),
            pl.BlockSpec((seq, width), per_group),
            pl.BlockSpec((seq, width), per_group),
            pl.BlockSpec((1, width), lambda b, g: (0, g)),
        ],
        out_specs=pl.BlockSpec((seq, width), per_group),
        out_shape=jax.ShapeDtypeStruct((batch * seq, A_WIDTH), BF16),
        compiler_params=pltpu.CompilerParams(
            dimension_semantics=("arbitrary", "arbitrary"), vmem_limit_bytes=_vmem_limit(est)),
        name="diffattn",
    )(lam_pack, q, k, v, g_head)


def _ffn_kernel(x_ref, m_ref, a_ref, wo_ref, g_ref, wg_ref, wu_ref, wd_ref, gf_ref, o_ref,
                *, final_norm):
    mix = (jnp.dot(m_ref[...], wo_ref[0:M_WIDTH, :], preferred_element_type=F32)
           + jnp.dot(a_ref[...], wo_ref[M_WIDTH:M_WIDTH + A_WIDTH, :],
                     preferred_element_type=F32))
    x1 = x_ref[...] + mix
    h2 = _rmsnorm(x1, g_ref[...], EPS).astype(BF16)
    acc = x1
    for lo, hi in FFN_CHUNKS:
        gate = jnp.dot(h2, wg_ref[:, lo:hi], preferred_element_type=F32)
        up = jnp.dot(h2, wu_ref[:, lo:hi], preferred_element_type=F32)
        act = (gate * jax.nn.sigmoid(gate) * up).astype(BF16)
        acc = acc + jnp.dot(act, wd_ref[lo:hi, :], preferred_element_type=F32)
    if final_norm:
        acc = _rmsnorm(acc, gf_ref[...], EPS)
    o_ref[...] = acc


def _ffn(x, m_out, a_out, w_out, g_ffn, w_gate, w_up, w_down, g_final, final_norm):
    rows = x.shape[0]
    tm = ROW_TILE
    row = lambda i: (i, 0)
    fixed = lambda i: (0, 0)
    resident = functools.partial(pl.BlockSpec, index_map=fixed, pipeline_mode=pl.Buffered(1))
    weights = 2 * (D_MODEL * D_MODEL + 3 * D_MODEL * FFN_HIDDEN)
    est = (weights + 2 * (2 * tm * D_MODEL * 4 + 2 * tm * M_WIDTH * 2)
           + 4 * tm * D_MODEL * 4 + 4 * tm * 768 * 4)
    return pl.pallas_call(
        functools.partial(_ffn_kernel, final_norm=final_norm),
        grid=(rows // tm,),
        in_specs=[
            pl.BlockSpec((tm, D_MODEL), row),
            pl.BlockSpec((tm, M_WIDTH), row),
            pl.BlockSpec((tm, A_WIDTH), row),
            resident((D_MODEL, D_MODEL)),
            pl.BlockSpec((1, D_MODEL), fixed),
            resident((D_MODEL, FFN_HIDDEN)),
            resident((D_MODEL, FFN_HIDDEN)),
            resident((FFN_HIDDEN, D_MODEL)),
            pl.BlockSpec((1, D_MODEL), fixed),
        ],
        out_specs=pl.BlockSpec((tm, D_MODEL), row),
        out_shape=jax.ShapeDtypeStruct((rows, D_MODEL), F32),
        compiler_params=pltpu.CompilerParams(
            dimension_semantics=("arbitrary",), vmem_limit_bytes=_vmem_limit(est)),
        name="ffn",
    )(x, m_out, a_out, w_out, g_ffn, w_gate, w_up, w_down, g_final)


def _rope_lane_tables(seq):
    pos = jnp.arange(seq, dtype=F32)
    inv = ROPE_THETA ** (-jnp.arange(0, A_QK_DIM, 2, dtype=F32) / A_QK_DIM)
    ang = pos[:, None] * inv[None, :]
    cos, sin = jnp.cos(ang), jnp.sin(ang)
    reps = LANES // A_QK_DIM
    cos_t = jnp.tile(cos, (1, 2 * reps))
    sin_t = jnp.tile(jnp.concatenate([-sin, sin], axis=1), (1, reps))
    return cos_t, sin_t


def _pad_lanes(a):
    return jnp.pad(a, ((0, 0), (0, LANES - a.shape[-1])))


def kernel(x, g_mix, w_in, conv_w, conv_b, b_gates, g_mlstm_head, lam_q1, lam_k1, lam_q2,
           lam_k2, g_diff_head, w_out, g_ffn, w_gate, w_up, w_down, g_final):
    batch, seq, d_model = x.shape
    depth = w_in.shape[0]
    assert d_model == D_MODEL and seq % ROW_TILE == 0 and seq % MLSTM_CHUNK == 0
    assert seq % ATTN_TQ == 0 and seq % CONV_TILE == 0
    cos_t, sin_t = _rope_lane_tables(seq)
    gate_lo = 4 * M_WIDTH
    gate_hi = gate_lo + N_GATES

    xf = x.reshape(batch * seq, d_model)
    for l in range(depth):
        w_l = w_in[l]
        w_main = jnp.concatenate([w_l[:, :gate_lo], w_l[:, gate_hi:]], axis=1).astype(BF16)
        w_g = _pad_lanes(w_l[:, gate_lo:gate_hi]).astype(BF16)
        b_g = _pad_lanes(b_gates[l][None, :])
        qk_m, v_m, o_m, gates, q_a, k_a, v_a = _inproj(
            xf, g_mix[l][None, :], w_main, w_g, b_g, cos_t, sin_t, seq)

        m_out = _mlstm(qk_m, v_m, o_m, gates, conv_w[l], conv_b[l][None, :],
                       g_mlstm_head[l][None, :], batch, seq)

        lam_init = 0.8 - 0.6 * math.exp(-0.3 * l)
        lam_rows = _pad_lanes(jnp.stack([lam_q1[l], lam_k1[l], lam_q2[l], lam_k2[l]]))
        lam_pack = jnp.concatenate(
            [lam_rows, jnp.full((1, LANES), lam_init, F32),
             jnp.zeros((SUBLANES - _LAM_ROWS - 1, LANES), F32)], axis=0)
        a_out = _diff_attention(lam_pack, q_a, k_a, v_a, g_diff_head[l][None, :], batch, seq)

        xf = _ffn(xf, m_out, a_out, w_out[l].astype(BF16), g_ffn[l][None, :],
                  w_gate[l].astype(BF16), w_up[l].astype(BF16), w_down[l].astype(BF16),
                  g_final[None, :], final_norm=(l == depth - 1))
    return xf.reshape(batch, seq, d_model)
```

```python
import functools
import math

import jax
import jax.numpy as jnp
from jax import lax
from jax.experimental import pallas as pl
from jax.experimental.pallas import tpu as pltpu

D_MODEL = 1024
M_WIDTH = 512
A_WIDTH = 512
M_HEADS = 4
A_HEADS = 4
HEAD_DIM = 128
A_QK_DIM = 64
CONV_K = 4
ROPE_THETA = 10000.0
FFN_HIDDEN = 2816
EPS = 1e-6
HEAD_EPS = 1e-5
N_GATES = 2 * M_HEADS

LANES = 128
SUBLANES = 8
V7X_VMEM_BYTES = 64 * 1024 * 1024

F32 = jnp.float32
BF16 = jnp.bfloat16

ROW_TILE = 512
MLSTM_CHUNK = 512
CONV_TILE = 512
ATTN_TQ = 512
ATTN_HEADS_PER_STEP = 2
FFN_CHUNKS = ((0, 768), (768, 1536), (1536, 2304), (2304, 2816))

_SEG_QK = (0, 2 * M_WIDTH)
_SEG_VM = (_SEG_QK[1], _SEG_QK[1] + M_WIDTH)
_SEG_OM = (_SEG_VM[1], _SEG_VM[1] + M_WIDTH)
_SEG_QA = (_SEG_OM[1], _SEG_OM[1] + A_WIDTH)
_SEG_KA = (_SEG_QA[1], _SEG_QA[1] + A_WIDTH)
_SEG_VA = (_SEG_KA[1], _SEG_KA[1] + A_WIDTH)
MAIN_WIDTH = _SEG_VA[1]

_LAM_ROWS = 4
_LAM_INIT_ROW = 4

_NT = (((1,), (1,)), ((), ()))


def _vmem_limit(estimate_bytes):
    return int(min(estimate_bytes * 5 // 4 + (8 << 20), V7X_VMEM_BYTES - (6 << 20)))


def _rmsnorm(x, g, eps):
    return x * lax.rsqrt(jnp.mean(x * x, axis=-1, keepdims=True) + eps) * g


def _log_sigmoid(x):
    return jnp.minimum(x, 0.0) - jnp.log1p(jnp.exp(-jnp.abs(x)))


def _rope_tile(t, cos, sin_signed, first_half):
    partner = jnp.where(first_half, pltpu.roll(t, LANES - 32, axis=1), pltpu.roll(t, 32, axis=1))
    return t * cos + partner * sin_signed


def _inproj_kernel(x_ref, g_ref, w_ref, wg_ref, bg_ref, cos_ref, sin_ref,
                   qk_ref, vm_ref, om_ref, gates_ref, qa_ref, ka_ref, va_ref):
    h = _rmsnorm(x_ref[...], g_ref[...], EPS).astype(BF16)

    def seg(bounds):
        return jnp.dot(h, w_ref[:, bounds[0]:bounds[1]], preferred_element_type=F32)

    cos = cos_ref[...]
    sin_signed = sin_ref[...]
    lane = lax.broadcasted_iota(jnp.int32, (1, LANES), 1)
    first_half = (lane % A_QK_DIM) < (A_QK_DIM // 2)
    q = seg(_SEG_QA)
    k = seg(_SEG_KA)
    scale = A_QK_DIM ** -0.5 * math.log2(math.e)
    for j in range(A_WIDTH // LANES):
        sl = slice(j * LANES, (j + 1) * LANES)
        qa_ref[:, sl] = (_rope_tile(q[:, sl], cos, sin_signed, first_half) * scale).astype(BF16)
        ka_ref[:, sl] = _rope_tile(k[:, sl], cos, sin_signed, first_half).astype(BF16)

    gates_ref[...] = jnp.dot(h, wg_ref[...], preferred_element_type=F32) + bg_ref[...]
    qk_ref[...] = seg(_SEG_QK).astype(BF16)
    vm_ref[...] = seg(_SEG_VM).astype(BF16)
    om_ref[...] = seg(_SEG_OM).astype(BF16)
    va_ref[...] = seg(_SEG_VA).astype(BF16)


def _of_layer(layer, *tail):
    return pl.BlockSpec((None,) + tail, lambda *_: (layer,) + (0,) * len(tail))


def _inproj(x, g, w_main, w_gate, b_gate, cos_t, sin_t, seq, layer):
    rows = x.shape[0]
    tm = ROW_TILE
    pos_blocks = seq // tm
    row = lambda i: (i, 0)
    pos = lambda i: (i % pos_blocks, 0)
    out_shapes = (
        jax.ShapeDtypeStruct((rows, 2 * M_WIDTH), BF16),
        jax.ShapeDtypeStruct((rows, M_WIDTH), BF16),
        jax.ShapeDtypeStruct((rows, M_WIDTH), BF16),
        jax.ShapeDtypeStruct((rows, LANES), F32),
        jax.ShapeDtypeStruct((rows, A_WIDTH), BF16),
        jax.ShapeDtypeStruct((rows, A_WIDTH), BF16),
        jax.ShapeDtypeStruct((rows, A_WIDTH), BF16),
    )
    out_specs = (
        pl.BlockSpec((tm, 2 * M_WIDTH), row),
        pl.BlockSpec((tm, M_WIDTH), row),
        pl.BlockSpec((tm, M_WIDTH), row),
        pl.BlockSpec((tm, LANES), row),
        pl.BlockSpec((tm, A_WIDTH), row),
        pl.BlockSpec((tm, A_WIDTH), row),
        pl.BlockSpec((tm, A_WIDTH), row),
    )
    est = (2 * (tm * D_MODEL * 4 + tm * MAIN_WIDTH * 2 + tm * LANES * 4 + 2 * tm * LANES * 4)
           + 2 * (D_MODEL * (MAIN_WIDTH + LANES) * 2) + 3 * tm * 2 * M_WIDTH * 4)
    return pl.pallas_call(
        _inproj_kernel,
        grid=(rows // tm,),
        in_specs=[
            pl.BlockSpec((tm, D_MODEL), row),
            _of_layer(layer, 1, D_MODEL),
            _of_layer(layer, D_MODEL, MAIN_WIDTH),
            _of_layer(layer, D_MODEL, LANES),
            _of_layer(layer, 1, LANES),
            pl.BlockSpec((tm, LANES), pos),
            pl.BlockSpec((tm, LANES), pos),
        ],
        out_specs=out_specs,
        out_shape=out_shapes,
        compiler_params=pltpu.CompilerParams(
            dimension_semantics=("arbitrary",), vmem_limit_bytes=_vmem_limit(est)),
        name="inproj",
    )(x, g, w_main, w_gate, b_gate, cos_t, sin_t)


def _lane_scan(x, lane, combine, identity):
    shift = 1
    while shift < LANES:
        x = combine(x, jnp.where(lane >= shift, pltpu.roll(x, shift, axis=1), identity))
        shift *= 2
    return x


def _group_scan(x, pos_in_group, group, combine, identity):
    shift = 1
    while shift < group:
        moved = pltpu.roll(x, SUBLANES * shift, axis=0)
        x = combine(x, jnp.where(pos_in_group >= shift, moved, identity))
        shift *= 2
    return x


def _mlstm_gate_terms(gate_ref, seq, chunk):
    n_tiles = seq // LANES
    per_chunk = chunk // LANES
    stacked = jnp.concatenate(
        [gate_ref[v * LANES:(v + 1) * LANES, :].T[0:SUBLANES, :] for v in range(n_tiles)], axis=0)
    shape = (n_tiles * SUBLANES, LANES)
    lane = lax.broadcasted_iota(jnp.int32, shape, 1)
    row = lax.broadcasted_iota(jnp.int32, shape, 0)
    is_i = (row % SUBLANES) < M_HEADS
    pos = (row // SUBLANES) % per_chunk

    log_f = jnp.where(is_i, 0.0, _log_sigmoid(stacked))
    in_tile = _lane_scan(log_f, lane, jnp.add, 0.0)
    tile_sum = jnp.broadcast_to(in_tile[:, LANES - 1:LANES], shape)
    bcum = in_tile + (_group_scan(tile_sum, pos, per_chunk, jnp.add, 0.0) - tile_sum)
    a = stacked - pltpu.roll(bcum, shape[0] - M_HEADS, axis=0)
    in_tile_max = _lane_scan(a, lane, jnp.maximum, -jnp.inf)
    tile_max = jnp.broadcast_to(in_tile_max[:, LANES - 1:LANES], shape)
    incl = _group_scan(tile_max, pos, per_chunk, jnp.maximum, -jnp.inf)
    before = jnp.where(pos >= 1, pltpu.roll(incl, SUBLANES, axis=0), -jnp.inf)
    cmax = jnp.maximum(in_tile_max, before)
    packed = jnp.where(is_i, cmax, bcum)
    pad_rows = jnp.zeros((LANES - SUBLANES, LANES), F32)
    cols = [jnp.concatenate([packed[v * SUBLANES:(v + 1) * SUBLANES, :], pad_rows], axis=0).T
            for v in range(n_tiles)]
    return a, cols


def _pv(p, v):
    half = p.shape[0] // 2
    return jnp.concatenate(
        [jnp.dot(p[:half], v, preferred_element_type=F32),
         jnp.dot(p[half:], v, preferred_element_type=F32)], axis=0)


def _mlstm_kernel(qk_ref, v_ref, o_ref, gate_ref, cw_ref, cb_ref, gh_ref, out_ref,
                  q_scr, k_scr, *, seq, chunk):
    L = chunk
    per_chunk = L // LANES

    off = SUBLANES - (CONV_K - 1)
    for t in range(seq // CONV_TILE):
        r = t * CONV_TILE
        cur = qk_ref[r:r + CONV_TILE, :].astype(F32)
        if t == 0:
            prev = jnp.zeros((SUBLANES, 2 * M_WIDTH), F32)
        else:
            prev = qk_ref[r - 2 * SUBLANES:r, :].astype(F32)[SUBLANES:2 * SUBLANES]
        ext = jnp.concatenate([prev, cur], axis=0)
        acc = cur * cw_ref[CONV_K - 1:CONV_K, :] + cb_ref[...]
        for j in range(CONV_K - 2, -1, -1):
            acc = acc + ext[off + j:off + j + CONV_TILE] * cw_ref[j:j + 1, :]
        qk = acc * jax.nn.sigmoid(acc)
        q_scr[r:r + CONV_TILE, :] = qk[:, :M_WIDTH].astype(BF16)
        k_scr[r:r + CONV_TILE, :] = (qk[:, M_WIDTH:] * (HEAD_DIM ** -0.5)).astype(BF16)

    a_all, cols = _mlstm_gate_terms(gate_ref, seq, L)
    tril = (lax.broadcasted_iota(jnp.int32, (L, L), 0)
            >= lax.broadcasted_iota(jnp.int32, (L, L), 1))
    ones_tile = (lax.broadcasted_iota(jnp.int32, (L, HEAD_DIM), 1) == 0).astype(BF16)

    for h in range(M_HEADS):
        hs = slice(h * HEAD_DIM, (h + 1) * HEAD_DIM)
        gh = gh_ref[:, hs]
        ct = None
        m_prev = jnp.zeros((1, 1), F32)
        for c in range(seq // L):
            r0 = c * L
            tiles = range(c * per_chunk, (c + 1) * per_chunk)
            a_row = jnp.concatenate(
                [a_all[v * SUBLANES + h:v * SUBLANES + h + 1, :] for v in tiles], axis=1)
            cmax_col = jnp.concatenate([cols[v][:, h:h + 1] for v in tiles], axis=0)
            b_col = jnp.concatenate(
                [cols[v][:, M_HEADS + h:M_HEADS + h + 1] for v in tiles], axis=0)

            big_m = jnp.maximum(m_prev, cmax_col)
            w_intra = jnp.exp(jnp.where(tril, a_row - big_m, -jnp.inf))
            qb = q_scr[r0:r0 + L, hs]
            kb = k_scr[r0:r0 + L, hs]
            v_aug = jnp.concatenate([v_ref[r0:r0 + L, hs], ones_tile], axis=1)
            sc = lax.dot_general(qb, kb, _NT, preferred_element_type=F32) * w_intra
            both = _pv(sc.astype(BF16), v_aug)
            if ct is not None:
                w_inter = jnp.exp(m_prev - big_m)
                both = both + w_inter * jnp.dot(qb, ct.astype(BF16), preferred_element_type=F32)
            num = both[:, :HEAD_DIM]
            den = both[:, HEAD_DIM:HEAD_DIM + 1]
            hh = num / jnp.maximum(jnp.abs(den), jnp.exp(-(b_col + big_m)))
            hn = _rmsnorm(hh, gh, HEAD_EPS)
            og = o_ref[r0:r0 + L, hs].astype(F32)
            out_ref[r0:r0 + L, hs] = (jax.nn.sigmoid(og) * hn).astype(BF16)

            if c + 1 < seq // L:
                m_last = big_m[L - 1:L, :]
                ws_row = jnp.exp(a_row - m_last)
                kw_t = (kb.astype(F32).T * ws_row).astype(BF16)
                update = jnp.dot(kw_t, v_aug, preferred_element_type=F32)
                ct = update if ct is None else jnp.exp(m_prev - m_last) * ct + update
                m_prev = b_col[L - 1:L, :] + m_last


def _mlstm(qk, v, o, gates, conv_w, conv_b, g_head, batch, seq, layer):
    per_b = lambda b: (b, 0)
    est = (2 * seq * (2 * M_WIDTH * 2 + 3 * M_WIDTH * 2 + LANES * 4) + 2 * seq * M_WIDTH * 2
           + 4 * CONV_TILE * 2 * M_WIDTH * 4 + 16 * MLSTM_CHUNK * MLSTM_CHUNK * 4)
    return pl.pallas_call(
        functools.partial(_mlstm_kernel, seq=seq, chunk=MLSTM_CHUNK),
        grid=(batch,),
        in_specs=[
            pl.BlockSpec((seq, 2 * M_WIDTH), per_b),
            pl.BlockSpec((seq, M_WIDTH), per_b),
            pl.BlockSpec((seq, M_WIDTH), per_b),
            pl.BlockSpec((seq, LANES), per_b),
            _of_layer(layer, CONV_K, 2 * M_WIDTH),
            _of_layer(layer, 1, 2 * M_WIDTH),
            _of_layer(layer, 1, M_WIDTH),
        ],
        out_specs=pl.BlockSpec((seq, M_WIDTH), per_b),
        out_shape=jax.ShapeDtypeStruct((batch * seq, M_WIDTH), BF16),
        scratch_shapes=[
            pltpu.VMEM((seq, M_WIDTH), BF16),
            pltpu.VMEM((seq, M_WIDTH), BF16),
        ],
        compiler_params=pltpu.CompilerParams(
            dimension_semantics=("arbitrary",), vmem_limit_bytes=_vmem_limit(est)),
        name="mlstm",
    )(qk, v, o, gates, conv_w, conv_b, g_head)


def _attn_kernel(lam_ref, q_ref, k_ref, v_ref, gh_ref, o_ref, *, seq, tq, heads):
    lane = lax.broadcasted_iota(jnp.int32, (1, LANES), 1)
    map_masks = tuple(((lane // A_QK_DIM) == c).astype(F32) for c in range(2))
    causal = (lax.broadcasted_iota(jnp.int32, (tq, tq), 1)
              <= lax.broadcasted_iota(jnp.int32, (tq, tq), 0))
    lp = lam_ref[...]
    lam_init = lp[_LAM_INIT_ROW:_LAM_INIT_ROW + 1, 0:1]
    lam = (jnp.exp(jnp.sum(lp[0:1] * lp[1:2], axis=-1, keepdims=True))
           - jnp.exp(jnp.sum(lp[2:3] * lp[3:4], axis=-1, keepdims=True)) + lam_init)
    ones_tile = (lax.broadcasted_iota(jnp.int32, (seq, HEAD_DIM), 1) == 0).astype(BF16)

    for j in range(heads):
        js = slice(j * HEAD_DIM, (j + 1) * HEAD_DIM)
        gh = gh_ref[:, js]
        v_aug = jnp.concatenate([v_ref[:, js], ones_tile], axis=1)
        for qi in range(seq // tq):
            r0 = qi * tq
            qf = q_ref[r0:r0 + tq, js].astype(F32)
            k_diag = k_ref[r0:r0 + tq, js]
            maps = []
            for c in range(2):
                qc = (qf * map_masks[c]).astype(BF16)
                s_d = jnp.where(causal,
                                lax.dot_general(qc, k_diag, _NT, preferred_element_type=F32),
                                -jnp.inf)
                m = jnp.max(s_d, axis=-1, keepdims=True)
                if qi > 0:
                    s_p = lax.dot_general(qc, k_ref[0:r0, js], _NT, preferred_element_type=F32)
                    m = jnp.maximum(m, jnp.max(s_p, axis=-1, keepdims=True))
                acc = _pv(jnp.exp2(s_d - m).astype(BF16), v_aug[r0:r0 + tq])
                if qi > 0:
                    acc = acc + _pv(jnp.exp2(s_p - m).astype(BF16), v_aug[0:r0])
                maps.append(acc[:, :HEAD_DIM] / acc[:, HEAD_DIM:HEAD_DIM + 1])
            o = maps[0] - lam * maps[1]
            o_ref[r0:r0 + tq, js] = (_rmsnorm(o, gh, HEAD_EPS) * (1.0 - lam_init)).astype(BF16)


def _diff_attention(lam_pack, q, k, v, g_head, batch, seq, layer):
    heads = ATTN_HEADS_PER_STEP
    width = heads * HEAD_DIM
    per_group = lambda b, g: (b, g)
    est = 2 * 4 * seq * width * 2 + 8 * heads * ATTN_TQ * seq * 4
    return pl.pallas_call(
        functools.partial(_attn_kernel, seq=seq, tq=ATTN_TQ, heads=heads),
        grid=(batch, A_HEADS // heads),
        in_specs=[
            _of_layer(layer, SUBLANES, LANES),
            pl.BlockSpec((seq, width), per_group),
            pl.BlockSpec((seq, width), per_group),
            pl.BlockSpec((seq, width), per_group),
            pl.BlockSpec((None, 1, width), lambda b, g: (layer, 0, g)),
        ],
        out_specs=pl.BlockSpec((seq, width), per_group),
        out_shape=jax.ShapeDtypeStruct((batch * seq, A_WIDTH), BF16),
        compiler_params=pltpu.CompilerParams(
            dimension_semantics=("arbitrary", "arbitrary"), vmem_limit_bytes=_vmem_limit(est)),
        name="diffattn",
    )(lam_pack, q, k, v, g_head)


def _ffn_kernel(x_ref, m_ref, a_ref, wo_ref, g_ref, wg_ref, wu_ref, wd_ref, gf_ref, o_ref,
                *, final_norm):
    mix = (jnp.dot(m_ref[...], wo_ref[0:M_WIDTH, :], preferred_element_type=F32)
           + jnp.dot(a_ref[...], wo_ref[M_WIDTH:M_WIDTH + A_WIDTH, :],
                     preferred_element_type=F32))
    x1 = x_ref[...] + mix
    h2 = _rmsnorm(x1, g_ref[...], EPS).astype(BF16)
    acc = x1
    for lo, hi in FFN_CHUNKS:
        gate = jnp.dot(h2, wg_ref[:, lo:hi], preferred_element_type=F32)
        up = jnp.dot(h2, wu_ref[:, lo:hi], preferred_element_type=F32)
        act = (gate * jax.nn.sigmoid(gate) * up).astype(BF16)
        acc = acc + jnp.dot(act, wd_ref[lo:hi, :], preferred_element_type=F32)
    if final_norm:
        acc = _rmsnorm(acc, gf_ref[...], EPS)
    o_ref[...] = acc


def _ffn(x, m_out, a_out, w_out, g_ffn, w_gate, w_up, w_down, g_final, final_norm, layer):
    rows = x.shape[0]
    tm = ROW_TILE
    row = lambda i: (i, 0)
    fixed = lambda i: (0, 0)

    def resident(*tail):
        return pl.BlockSpec((None,) + tail, lambda i: (layer,) + (0,) * len(tail),
                            pipeline_mode=pl.Buffered(1))

    weights = 2 * (D_MODEL * D_MODEL + 3 * D_MODEL * FFN_HIDDEN)
    est = (weights + 2 * (2 * tm * D_MODEL * 4 + 2 * tm * M_WIDTH * 2)
           + 4 * tm * D_MODEL * 4 + 4 * tm * 768 * 4)
    return pl.pallas_call(
        functools.partial(_ffn_kernel, final_norm=final_norm),
        grid=(rows // tm,),
        in_specs=[
            pl.BlockSpec((tm, D_MODEL), row),
            pl.BlockSpec((tm, M_WIDTH), row),
            pl.BlockSpec((tm, A_WIDTH), row),
            resident(D_MODEL, D_MODEL),
            _of_layer(layer, 1, D_MODEL),
            resident(D_MODEL, FFN_HIDDEN),
            resident(D_MODEL, FFN_HIDDEN),
            resident(FFN_HIDDEN, D_MODEL),
            pl.BlockSpec((1, D_MODEL), fixed),
        ],
        out_specs=pl.BlockSpec((tm, D_MODEL), row),
        out_shape=jax.ShapeDtypeStruct((rows, D_MODEL), F32),
        compiler_params=pltpu.CompilerParams(
            dimension_semantics=("arbitrary",), vmem_limit_bytes=_vmem_limit(est)),
        name="ffn",
    )(x, m_out, a_out, w_out, g_ffn, w_gate, w_up, w_down, g_final)


def _rope_lane_tables(seq):
    pos = jnp.arange(seq, dtype=F32)
    inv = ROPE_THETA ** (-jnp.arange(0, A_QK_DIM, 2, dtype=F32) / A_QK_DIM)
    ang = pos[:, None] * inv[None, :]
    cos, sin = jnp.cos(ang), jnp.sin(ang)
    reps = LANES // A_QK_DIM
    cos_t = jnp.tile(cos, (1, 2 * reps))
    sin_t = jnp.tile(jnp.concatenate([-sin, sin], axis=1), (1, reps))
    return cos_t, sin_t


def _pad_lanes(a):
    return jnp.pad(a, ((0, 0),) * (a.ndim - 1) + ((0, LANES - a.shape[-1]),))


def kernel(x, g_mix, w_in, conv_w, conv_b, b_gates, g_mlstm_head, lam_q1, lam_k1, lam_q2,
           lam_k2, g_diff_head, w_out, g_ffn, w_gate, w_up, w_down, g_final):
    batch, seq, d_model = x.shape
    depth = w_in.shape[0]
    assert d_model == D_MODEL and seq % ROW_TILE == 0 and seq % MLSTM_CHUNK == 0
    assert seq % ATTN_TQ == 0 and seq % CONV_TILE == 0
    cos_t, sin_t = _rope_lane_tables(seq)
    gate_lo = 4 * M_WIDTH
    gate_hi = gate_lo + N_GATES

    w_main = jnp.concatenate([w_in[:, :, :gate_lo], w_in[:, :, gate_hi:]], axis=2).astype(BF16)
    w_g = _pad_lanes(w_in[:, :, gate_lo:gate_hi]).astype(BF16)
    b_g = _pad_lanes(b_gates)[:, None, :]
    w_out_b, w_gate_b, w_up_b, w_down_b = (w.astype(BF16) for w in (w_out, w_gate, w_up, w_down))
    lam_inits = jnp.asarray([0.8 - 0.6 * math.exp(-0.3 * l) for l in range(depth)], F32)
    lam_pack = jnp.concatenate(
        [_pad_lanes(jnp.stack([lam_q1, lam_k1, lam_q2, lam_k2], axis=1)),
         jnp.broadcast_to(lam_inits[:, None, None], (depth, 1, LANES)),
         jnp.zeros((depth, SUBLANES - _LAM_ROWS - 1, LANES), F32)], axis=1)
    row_vec = lambda a: a[:, None, :]

    xf = x.reshape(batch * seq, d_model)
    for l in range(depth):
        qk_m, v_m, o_m, gates, q_a, k_a, v_a = _inproj(
            xf, row_vec(g_mix), w_main, w_g, b_g, cos_t, sin_t, seq, l)
        m_out = _mlstm(qk_m, v_m, o_m, gates, conv_w, row_vec(conv_b), row_vec(g_mlstm_head),
                       batch, seq, l)
        a_out = _diff_attention(lam_pack, q_a, k_a, v_a, row_vec(g_diff_head), batch, seq, l)
        xf = _ffn(xf, m_out, a_out, w_out_b, row_vec(g_ffn), w_gate_b, w_up_b, w_down_b,
                  g_final[None, :], final_norm=(l == depth - 1), layer=l)
    return xf.reshape(batch, seq, d_model)
```

```python
import functools
import math

import jax
import jax.numpy as jnp
from jax import lax
from jax.experimental import pallas as pl
from jax.experimental.pallas import tpu as pltpu

D_MODEL = 1024
M_WIDTH = 512
A_WIDTH = 512
M_HEADS = 4
A_HEADS = 4
HEAD_DIM = 128
A_QK_DIM = 64
CONV_K = 4
ROPE_THETA = 10000.0
FFN_HIDDEN = 2816
EPS = 1e-6
HEAD_EPS = 1e-5
N_GATES = 2 * M_HEADS

LANES = 128
SUBLANES = 8
V7X_VMEM_BYTES = 64 * 1024 * 1024

F32 = jnp.float32
BF16 = jnp.bfloat16

ROW_TILE = 512
MLSTM_CHUNK = 512
CONV_TILE = 512
ATTN_TQ = 512
ATTN_HEADS_PER_STEP = 2
FFN_CHUNKS = ((0, 768), (768, 1536), (1536, 2304), (2304, 2816))

_SEG_QK = (0, 2 * M_WIDTH)
_SEG_VM = (_SEG_QK[1], _SEG_QK[1] + M_WIDTH)
_SEG_OM = (_SEG_VM[1], _SEG_VM[1] + M_WIDTH)
_SEG_QA = (_SEG_OM[1], _SEG_OM[1] + A_WIDTH)
_SEG_KA = (_SEG_QA[1], _SEG_QA[1] + A_WIDTH)
_SEG_VA = (_SEG_KA[1], _SEG_KA[1] + A_WIDTH)
MAIN_WIDTH = _SEG_VA[1]

_LAM_ROWS = 4
_LAM_INIT_ROW = 4

_NT = (((1,), (1,)), ((), ()))


def _vmem_limit(estimate_bytes):
    return int(min(estimate_bytes * 5 // 4 + (8 << 20), V7X_VMEM_BYTES - (6 << 20)))


def _rmsnorm(x, g, eps):
    return x * lax.rsqrt(jnp.mean(x * x, axis=-1, keepdims=True) + eps) * g


def _log_sigmoid(x):
    return jnp.minimum(x, 0.0) - jnp.log1p(jnp.exp(-jnp.abs(x)))


def _rope_tile(t, cos, sin_signed, first_half):
    partner = jnp.where(first_half, pltpu.roll(t, LANES - 32, axis=1), pltpu.roll(t, 32, axis=1))
    return t * cos + partner * sin_signed


def _inproj_kernel(x_ref, g_ref, w_ref, wg_ref, bg_ref, cos_ref, sin_ref,
                   qk_ref, vm_ref, om_ref, gates_ref, qa_ref, ka_ref, va_ref):
    h = _rmsnorm(x_ref[...], g_ref[...], EPS).astype(BF16)

    def seg(bounds):
        return jnp.dot(h, w_ref[:, bounds[0]:bounds[1]], preferred_element_type=F32)

    cos = cos_ref[...]
    sin_signed = sin_ref[...]
    lane = lax.broadcasted_iota(jnp.int32, (1, LANES), 1)
    first_half = (lane % A_QK_DIM) < (A_QK_DIM // 2)
    q = seg(_SEG_QA)
    k = seg(_SEG_KA)
    scale = A_QK_DIM ** -0.5 * math.log2(math.e)
    for j in range(A_WIDTH // LANES):
        sl = slice(j * LANES, (j + 1) * LANES)
        qa_ref[:, sl] = (_rope_tile(q[:, sl], cos, sin_signed, first_half) * scale).astype(BF16)
        ka_ref[:, sl] = _rope_tile(k[:, sl], cos, sin_signed, first_half).astype(BF16)

    gates_ref[...] = jnp.dot(h, wg_ref[...], preferred_element_type=F32) + bg_ref[...]
    qk_ref[...] = seg(_SEG_QK).astype(BF16)
    vm_ref[...] = seg(_SEG_VM).astype(BF16)
    om_ref[...] = seg(_SEG_OM).astype(BF16)
    va_ref[...] = seg(_SEG_VA).astype(BF16)


def _of_layer(layer, *tail):
    return pl.BlockSpec((None,) + tail, lambda *_: (layer,) + (0,) * len(tail))


def _inproj(x, g, w_main, w_gate, b_gate, cos_t, sin_t, seq, layer):
    rows = x.shape[0]
    tm = ROW_TILE
    pos_blocks = seq // tm
    row = lambda i: (i, 0)
    pos = lambda i: (i % pos_blocks, 0)
    out_shapes = (
        jax.ShapeDtypeStruct((rows, 2 * M_WIDTH), BF16),
        jax.ShapeDtypeStruct((rows, M_WIDTH), BF16),
        jax.ShapeDtypeStruct((rows, M_WIDTH), BF16),
        jax.ShapeDtypeStruct((rows, LANES), F32),
        jax.ShapeDtypeStruct((rows, A_WIDTH), BF16),
        jax.ShapeDtypeStruct((rows, A_WIDTH), BF16),
        jax.ShapeDtypeStruct((rows, A_WIDTH), BF16),
    )
    out_specs = (
        pl.BlockSpec((tm, 2 * M_WIDTH), row),
        pl.BlockSpec((tm, M_WIDTH), row),
        pl.BlockSpec((tm, M_WIDTH), row),
        pl.BlockSpec((tm, LANES), row),
        pl.BlockSpec((tm, A_WIDTH), row),
        pl.BlockSpec((tm, A_WIDTH), row),
        pl.BlockSpec((tm, A_WIDTH), row),
    )
    est = (2 * (tm * D_MODEL * 4 + tm * MAIN_WIDTH * 2 + tm * LANES * 4 + 2 * tm * LANES * 4)
           + 2 * (D_MODEL * (MAIN_WIDTH + LANES) * 2) + 3 * tm * 2 * M_WIDTH * 4)
    return pl.pallas_call(
        _inproj_kernel,
        grid=(rows // tm,),
        in_specs=[
            pl.BlockSpec((tm, D_MODEL), row),
            _of_layer(layer, 1, D_MODEL),
            _of_layer(layer, D_MODEL, MAIN_WIDTH),
            _of_layer(layer, D_MODEL, LANES),
            _of_layer(layer, 1, LANES),
            pl.BlockSpec((tm, LANES), pos),
            pl.BlockSpec((tm, LANES), pos),
        ],
        out_specs=out_specs,
        out_shape=out_shapes,
        compiler_params=pltpu.CompilerParams(
            dimension_semantics=("arbitrary",), vmem_limit_bytes=_vmem_limit(est)),
        name="inproj",
    )(x, g, w_main, w_gate, b_gate, cos_t, sin_t)


def _lane_scan(x, lane, combine, identity):
    shift = 1
    while shift < LANES:
        x = combine(x, jnp.where(lane >= shift, pltpu.roll(x, shift, axis=1), identity))
        shift *= 2
    return x


def _group_scan(x, pos_in_group, group, combine, identity):
    shift = 1
    while shift < group:
        moved = pltpu.roll(x, SUBLANES * shift, axis=0)
        x = combine(x, jnp.where(pos_in_group >= shift, moved, identity))
        shift *= 2
    return x


def _mlstm_gate_terms(gate_ref, seq, chunk):
    n_tiles = seq // LANES
    per_chunk = chunk // LANES
    stacked = jnp.concatenate(
        [gate_ref[v * LANES:(v + 1) * LANES, :].T[0:SUBLANES, :] for v in range(n_tiles)], axis=0)
    shape = (n_tiles * SUBLANES, LANES)
    lane = lax.broadcasted_iota(jnp.int32, shape, 1)
    row = lax.broadcasted_iota(jnp.int32, shape, 0)
    is_i = (row % SUBLANES) < M_HEADS
    pos = (row // SUBLANES) % per_chunk

    log_f = jnp.where(is_i, 0.0, _log_sigmoid(stacked))
    in_tile = _lane_scan(log_f, lane, jnp.add, 0.0)
    tile_sum = jnp.broadcast_to(in_tile[:, LANES - 1:LANES], shape)
    bcum = in_tile + (_group_scan(tile_sum, pos, per_chunk, jnp.add, 0.0) - tile_sum)
    a = stacked - pltpu.roll(bcum, shape[0] - M_HEADS, axis=0)
    in_tile_max = _lane_scan(a, lane, jnp.maximum, -jnp.inf)
    tile_max = jnp.broadcast_to(in_tile_max[:, LANES - 1:LANES], shape)
    incl = _group_scan(tile_max, pos, per_chunk, jnp.maximum, -jnp.inf)
    before = jnp.where(pos >= 1, pltpu.roll(incl, SUBLANES, axis=0), -jnp.inf)
    cmax = jnp.maximum(in_tile_max, before)
    packed = jnp.where(is_i, cmax, bcum)
    pad_rows = jnp.zeros((LANES - SUBLANES, LANES), F32)
    cols = [jnp.concatenate([packed[v * SUBLANES:(v + 1) * SUBLANES, :], pad_rows], axis=0).T
            for v in range(n_tiles)]
    return a, cols


def _pv(p, v):
    half = p.shape[0] // 2
    return jnp.concatenate(
        [jnp.dot(p[:half], v, preferred_element_type=F32),
         jnp.dot(p[half:], v, preferred_element_type=F32)], axis=0)


def _mlstm_kernel(qk_ref, v_ref, o_ref, gate_ref, cw_ref, cb_ref, gh_ref, out_ref,
                  q_scr, k_scr, *, seq, chunk):
    L = chunk
    per_chunk = L // LANES

    off = SUBLANES - (CONV_K - 1)
    for t in range(seq // CONV_TILE):
        r = t * CONV_TILE
        cur = qk_ref[r:r + CONV_TILE, :].astype(F32)
        if t == 0:
            prev = jnp.zeros((SUBLANES, 2 * M_WIDTH), F32)
        else:
            prev = qk_ref[r - 2 * SUBLANES:r, :].astype(F32)[SUBLANES:2 * SUBLANES]
        ext = jnp.concatenate([prev, cur], axis=0)
        acc = cur * cw_ref[CONV_K - 1:CONV_K, :] + cb_ref[...]
        for j in range(CONV_K - 2, -1, -1):
            acc = acc + ext[off + j:off + j + CONV_TILE] * cw_ref[j:j + 1, :]
        qk = acc * jax.nn.sigmoid(acc)
        q_scr[r:r + CONV_TILE, :] = qk[:, :M_WIDTH].astype(BF16)
        k_scr[r:r + CONV_TILE, :] = (qk[:, M_WIDTH:] * (HEAD_DIM ** -0.5)).astype(BF16)

    a_all, cols = _mlstm_gate_terms(gate_ref, seq, L)
    tril = (lax.broadcasted_iota(jnp.int32, (L, L), 0)
            >= lax.broadcasted_iota(jnp.int32, (L, L), 1))
    ones_tile = (lax.broadcasted_iota(jnp.int32, (L, HEAD_DIM), 1) == 0).astype(BF16)

    for h in range(M_HEADS):
        hs = slice(h * HEAD_DIM, (h + 1) * HEAD_DIM)
        gh = gh_ref[:, hs]
        ct = None
        m_prev = jnp.zeros((1, 1), F32)
        for c in range(seq // L):
            r0 = c * L
            tiles = range(c * per_chunk, (c + 1) * per_chunk)
            a_row = jnp.concatenate(
                [a_all[v * SUBLANES + h:v * SUBLANES + h + 1, :] for v in tiles], axis=1)
            cmax_col = jnp.concatenate([cols[v][:, h:h + 1] for v in tiles], axis=0)
            b_col = jnp.concatenate(
                [cols[v][:, M_HEADS + h:M_HEADS + h + 1] for v in tiles], axis=0)

            big_m = jnp.maximum(m_prev, cmax_col)
            w_intra = jnp.exp(jnp.where(tril, a_row - big_m, -jnp.inf))
            qb = q_scr[r0:r0 + L, hs]
            kb = k_scr[r0:r0 + L, hs]
            v_aug = jnp.concatenate([v_ref[r0:r0 + L, hs], ones_tile], axis=1)
            sc = lax.dot_general(qb, kb, _NT, preferred_element_type=F32) * w_intra
            both = _pv(sc.astype(BF16), v_aug)
            if ct is not None:
                w_inter = jnp.exp(m_prev - big_m)
                both = both + w_inter * jnp.dot(qb, ct.astype(BF16), preferred_element_type=F32)
            num = both[:, :HEAD_DIM]
            den = both[:, HEAD_DIM:HEAD_DIM + 1]
            hh = num / jnp.maximum(jnp.abs(den), jnp.exp(-(b_col + big_m)))
            hn = _rmsnorm(hh, gh, HEAD_EPS)
            og = o_ref[r0:r0 + L, hs].astype(F32)
            out_ref[r0:r0 + L, hs] = (jax.nn.sigmoid(og) * hn).astype(BF16)

            if c + 1 < seq // L:
                m_last = big_m[L - 1:L, :]
                ws_row = jnp.exp(a_row - m_last)
                kw_t = (kb.astype(F32).T * ws_row).astype(BF16)
                update = jnp.dot(kw_t, v_aug, preferred_element_type=F32)
                ct = update if ct is None else jnp.exp(m_prev - m_last) * ct + update
                m_prev = b_col[L - 1:L, :] + m_last


def _mlstm(qk, v, o, gates, conv_w, conv_b, g_head, batch, seq, layer):
    per_b = lambda b: (b, 0)
    est = (2 * seq * (2 * M_WIDTH * 2 + 3 * M_WIDTH * 2 + LANES * 4) + 2 * seq * M_WIDTH * 2
           + 4 * CONV_TILE * 2 * M_WIDTH * 4 + 16 * MLSTM_CHUNK * MLSTM_CHUNK * 4)
    return pl.pallas_call(
        functools.partial(_mlstm_kernel, seq=seq, chunk=MLSTM_CHUNK),
        grid=(batch,),
        in_specs=[
            pl.BlockSpec((seq, 2 * M_WIDTH), per_b),
            pl.BlockSpec((seq, M_WIDTH), per_b),
            pl.BlockSpec((seq, M_WIDTH), per_b),
            pl.BlockSpec((seq, LANES), per_b),
            _of_layer(layer, CONV_K, 2 * M_WIDTH),
            _of_layer(layer, 1, 2 * M_WIDTH),
            _of_layer(layer, 1, M_WIDTH),
        ],
        out_specs=pl.BlockSpec((seq, M_WIDTH), per_b),
        out_shape=jax.ShapeDtypeStruct((batch * seq, M_WIDTH), BF16),
        scratch_shapes=[
            pltpu.VMEM((seq, M_WIDTH), BF16),
            pltpu.VMEM((seq, M_WIDTH), BF16),
        ],
        compiler_params=pltpu.CompilerParams(
            dimension_semantics=("arbitrary",), vmem_limit_bytes=_vmem_limit(est)),
        name="mlstm",
    )(qk, v, o, gates, conv_w, conv_b, g_head)


def _attn_kernel(lam_ref, q_ref, k_ref, v_ref, gh_ref, o_ref, *, seq, tq, heads):
    lane = lax.broadcasted_iota(jnp.int32, (1, LANES), 1)
    map_masks = tuple(((lane // A_QK_DIM) == c).astype(F32) for c in range(2))
    causal = (lax.broadcasted_iota(jnp.int32, (tq, tq), 1)
              <= lax.broadcasted_iota(jnp.int32, (tq, tq), 0))
    lp = lam_ref[...]
    lam_init = lp[_LAM_INIT_ROW:_LAM_INIT_ROW + 1, 0:1]
    lam = (jnp.exp(jnp.sum(lp[0:1] * lp[1:2], axis=-1, keepdims=True))
           - jnp.exp(jnp.sum(lp[2:3] * lp[3:4], axis=-1, keepdims=True)) + lam_init)
    ones_tile = (lax.broadcasted_iota(jnp.int32, (seq, HEAD_DIM), 1) == 0).astype(BF16)

    for j in range(heads):
        js = slice(j * HEAD_DIM, (j + 1) * HEAD_DIM)
        gh = gh_ref[:, js]
        v_aug = jnp.concatenate([v_ref[:, js], ones_tile], axis=1)
        for qi in range(seq // tq):
            r0 = qi * tq
            qf = q_ref[r0:r0 + tq, js].astype(F32)
            k_diag = k_ref[r0:r0 + tq, js]
            maps = []
            for c in range(2):
                qc = (qf * map_masks[c]).astype(BF16)
                s_d = jnp.where(causal,
                                lax.dot_general(qc, k_diag, _NT, preferred_element_type=F32),
                                -jnp.inf)
                m = jnp.max(s_d, axis=-1, keepdims=True)
                if qi > 0:
                    s_p = lax.dot_general(qc, k_ref[0:r0, js], _NT, preferred_element_type=F32)
                    m = jnp.maximum(m, jnp.max(s_p, axis=-1, keepdims=True))
                acc = _pv(jnp.exp2(s_d - m).astype(BF16), v_aug[r0:r0 + tq])
                if qi > 0:
                    acc = acc + _pv(jnp.exp2(s_p - m).astype(BF16), v_aug[0:r0])
                maps.append(acc[:, :HEAD_DIM] / acc[:, HEAD_DIM:HEAD_DIM + 1])
            o = maps[0] - lam * maps[1]
            o_ref[r0:r0 + tq, js] = (_rmsnorm(o, gh, HEAD_EPS) * (1.0 - lam_init)).astype(BF16)


def _diff_attention(lam_pack, q, k, v, g_head, batch, seq, layer):
    heads = ATTN_HEADS_PER_STEP
    width = heads * HEAD_DIM
    per_group = lambda b, g: (b, g)
    est = 2 * 4 * seq * width * 2 + 8 * heads * ATTN_TQ * seq * 4
    return pl.pallas_call(
        functools.partial(_attn_kernel, seq=seq, tq=ATTN_TQ, heads=heads),
        grid=(batch, A_HEADS // heads),
        in_specs=[
            _of_layer(layer, SUBLANES, LANES),
            pl.BlockSpec((seq, width), per_group),
            pl.BlockSpec((seq, width), per_group),
            pl.BlockSpec((seq, width), per_group),
            pl.BlockSpec((None, 1, width), lambda b, g: (layer, 0, g)),
        ],
        out_specs=pl.BlockSpec((seq, width), per_group),
        out_shape=jax.ShapeDtypeStruct((batch * seq, A_WIDTH), BF16),
        compiler_params=pltpu.CompilerParams(
            dimension_semantics=("arbitrary", "arbitrary"), vmem_limit_bytes=_vmem_limit(est)),
        name="diffattn",
    )(lam_pack, q, k, v, g_head)


def _ffn_kernel(x_ref, m_ref, a_ref, wo_ref, g_ref, wg_ref, wu_ref, wd_ref, gf_ref, o_ref,
                *, final_norm):
    mix = (jnp.dot(m_ref[...], wo_ref[0:M_WIDTH, :], preferred_element_type=F32)
           + jnp.dot(a_ref[...], wo_ref[M_WIDTH:M_WIDTH + A_WIDTH, :],
                     preferred_element_type=F32))
    x1 = x_ref[...] + mix
    h2 = _rmsnorm(x1, g_ref[...], EPS).astype(BF16)
    acc = x1
    for lo, hi in FFN_CHUNKS:
        gate = jnp.dot(h2, wg_ref[:, lo:hi], preferred_element_type=F32)
        up = jnp.dot(h2, wu_ref[:, lo:hi], preferred_element_type=F32)
        act = (gate * jax.nn.sigmoid(gate) * up).astype(BF16)
        acc = acc + jnp.dot(act, wd_ref[lo:hi, :], preferred_element_type=F32)
    if final_norm:
        acc = _rmsnorm(acc, gf_ref[...], EPS)
    o_ref[...] = acc


def _ffn(x, m_out, a_out, w_out, g_ffn, w_gate, w_up, w_down, g_final, final_norm, layer):
    rows = x.shape[0]
    tm = ROW_TILE
    row = lambda i: (i, 0)
    fixed = lambda i: (0, 0)

    def resident(*tail):
        return pl.BlockSpec((None,) + tail, lambda i: (layer,) + (0,) * len(tail),
                            pipeline_mode=pl.Buffered(1))

    weights = 2 * (D_MODEL * D_MODEL + 3 * D_MODEL * FFN_HIDDEN)
    est = (weights + 2 * (2 * tm * D_MODEL * 4 + 2 * tm * M_WIDTH * 2)
           + 4 * tm * D_MODEL * 4 + 4 * tm * 768 * 4)
    return pl.pallas_call(
        functools.partial(_ffn_kernel, final_norm=final_norm),
        grid=(rows // tm,),
        in_specs=[
            pl.BlockSpec((tm, D_MODEL), row),
            pl.BlockSpec((tm, M_WIDTH), row),
            pl.BlockSpec((tm, A_WIDTH), row),
            resident(D_MODEL, D_MODEL),
            _of_layer(layer, 1, D_MODEL),
            resident(D_MODEL, FFN_HIDDEN),
            resident(D_MODEL, FFN_HIDDEN),
            resident(FFN_HIDDEN, D_MODEL),
            pl.BlockSpec((1, D_MODEL), fixed),
        ],
        out_specs=pl.BlockSpec((tm, D_MODEL), row),
        out_shape=jax.ShapeDtypeStruct((rows, D_MODEL), F32),
        compiler_params=pltpu.CompilerParams(
            dimension_semantics=("arbitrary",), vmem_limit_bytes=_vmem_limit(est)),
        name="ffn",
    )(x, m_out, a_out, w_out, g_ffn, w_gate, w_up, w_down, g_final)


def _rope_lane_tables(seq):
    pos = jnp.arange(seq, dtype=F32)
    inv = ROPE_THETA ** (-jnp.arange(0, A_QK_DIM, 2, dtype=F32) / A_QK_DIM)
    ang = pos[:, None] * inv[None, :]
    cos, sin = jnp.cos(ang), jnp.sin(ang)
    reps = LANES // A_QK_DIM
    cos_t = jnp.tile(cos, (1, 2 * reps))
    sin_t = jnp.tile(jnp.concatenate([-sin, sin], axis=1), (1, reps))
    return cos_t, sin_t


def _pad_lanes(a):
    return jnp.pad(a, ((0, 0),) * (a.ndim - 1) + ((0, LANES - a.shape[-1]),))


def kernel(x, g_mix, w_in, conv_w, conv_b, b_gates, g_mlstm_head, lam_q1, lam_k1, lam_q2,
           lam_k2, g_diff_head, w_out, g_ffn, w_gate, w_up, w_down, g_final):
    batch, seq, d_model = x.shape
    depth = w_in.shape[0]
    assert d_model == D_MODEL and seq % ROW_TILE == 0 and seq % MLSTM_CHUNK == 0
    assert seq % ATTN_TQ == 0 and seq % CONV_TILE == 0
    cos_t, sin_t = _rope_lane_tables(seq)
    gate_lo = 4 * M_WIDTH
    gate_hi = gate_lo + N_GATES

    w_in_b = w_in.astype(BF16)
    w_main = jnp.concatenate([w_in_b[:, :, :gate_lo], w_in_b[:, :, gate_hi:]], axis=2)
    w_g = _pad_lanes(w_in_b[:, :, gate_lo:gate_hi])
    b_g = _pad_lanes(b_gates)[:, None, :]
    w_out_b, w_gate_b, w_up_b, w_down_b = (w.astype(BF16) for w in (w_out, w_gate, w_up, w_down))
    lam_inits = jnp.asarray([0.8 - 0.6 * math.exp(-0.3 * l) for l in range(depth)], F32)
    lam_pack = jnp.concatenate(
        [_pad_lanes(jnp.stack([lam_q1, lam_k1, lam_q2, lam_k2], axis=1)),
         jnp.broadcast_to(lam_inits[:, None, None], (depth, 1, LANES)),
         jnp.zeros((depth, SUBLANES - _LAM_ROWS - 1, LANES), F32)], axis=1)
    row_vec = lambda a: a[:, None, :]

    xf = x.reshape(batch * seq, d_model)
    for l in range(depth):
        qk_m, v_m, o_m, gates, q_a, k_a, v_a = _inproj(
            xf, row_vec(g_mix), w_main, w_g, b_g, cos_t, sin_t, seq, l)
        m_out = _mlstm(qk_m, v_m, o_m, gates, conv_w, row_vec(conv_b), row_vec(g_mlstm_head),
                       batch, seq, l)
        a_out = _diff_attention(lam_pack, q_a, k_a, v_a, row_vec(g_diff_head), batch, seq, l)
        xf = _ffn(xf, m_out, a_out, w_out_b, row_vec(g_ffn), w_gate_b, w_up_b, w_down_b,
                  g_final[None, :], final_norm=(l == depth - 1), layer=l)
    return xf.reshape(batch, seq, d_model)
```
